```python
import math
import jax
import jax.numpy as jnp
from jax import lax
import numpy as np

D_MODEL = 1024
BATCH = 4
SEQ = 4096
DEPTH = 1

N_META = 16
D_MIX = 2 * D_MODEL
EPS = 1e-6
HG_WIDTH = D_MIX // 2
HG_DK = 128
HG_HEADS = HG_WIDTH // 128
HG_DV = HG_WIDTH // HG_HEADS
HG_KEY = HG_HEADS * HG_DK
HG_CHUNK = 16
M2_WIDTH = D_MIX - HG_WIDTH
M2_HEADDIM = 64
M2_HEADS = M2_WIDTH // M2_HEADDIM
M2_STATE = 128
M2_GROUPS = 2
M2_HPG = M2_HEADS // M2_GROUPS
M2_GN = M2_GROUPS * M2_STATE
M2_XBC = M2_WIDTH + 2 * M2_GN
M2_CONV = 4
M2_CHUNK = 64
DT_MIN = 1e-3
DT_MAX = 1e-1
D_FF = 2816
FFN_CONV = 3
PROJ_SIZES = (HG_KEY, HG_KEY, HG_WIDTH, HG_WIDTH, M2_WIDTH, M2_XBC, M2_HEADS)
D_PROJ = sum(PROJ_SIZES)
PROJ_SPLITS = tuple(int(v) for v in np.cumsum(PROJ_SIZES)[:-1])

kernel_name = 'hybrid_hgrn2_mamba2_convffn'


def rmsnorm(x, w):
    xf = x.astype(jnp.float32)
    y = xf * lax.rsqrt(jnp.mean(xf * xf, axis=-1, keepdims=True) + EPS)
    return (y * w.astype(jnp.float32)).astype(x.dtype)


def causal_dwconv(x, w, b):
    k_taps, length = w.shape[0], x.shape[1]
    xp = jnp.pad(x, ((0, 0), (k_taps - 1, 0), (0, 0)))
    y = b + xp[:, 0:length] * w[0]
    for k in range(1, k_taps):
        y = y + xp[:, k:k + length] * w[k]
    return y


def front_pad(t, n):
    return jnp.pad(t, ((0, 0), (n, 0)) + ((0, 0),) * (t.ndim - 2))


def to_chunks(t, c):
    return t.reshape((t.shape[0], t.shape[1] // c, c) + t.shape[2:])


def hgrn2_mixer(q, f_pre, i_in, g, lb, norm_w):
    bsz, length, _ = q.shape
    f32 = jnp.float32
    f = lb + (1.0 - lb) * jax.nn.sigmoid(f_pre.astype(f32))
    k = 1.0 - f
    log_f = jnp.log(f)
    qa = jax.nn.silu(q.astype(f32))
    pad = (-N_META) % HG_CHUNK
    lp = length + pad

    def prep(t, d):
        return to_chunks(front_pad(t, pad).reshape(bsz, lp, HG_HEADS, d), HG_CHUNK)

    qc, kc, gc = prep(qa, HG_DK), prep(k, HG_DK), prep(log_f, HG_DK)
    vc = prep(i_in.astype(f32), HG_DV)
    bcum = jnp.cumsum(gc, axis=2)
    blast = bcum[:, :, -1]
    q_dec = qc * jnp.exp(bcum)
    k_inv = kc * jnp.exp(-bcum)
    k_end = kc * jnp.exp(blast[:, :, None] - bcum)
    causal = jnp.tril(jnp.ones((HG_CHUNK, HG_CHUNK), dtype=bool))
    scores = jnp.einsum('bnrhk,bnshk->bnhrs', q_dec, k_inv)
    scores = jnp.where(causal, scores, 0.0)
    o_intra = jnp.einsum('bnhrs,bnshv->bnrhv', scores, vc)

    def step(state, inp):
        q_n, k_n, v_n, d_n = inp
        o_n = jnp.einsum('brhk,bhkv->brhv', q_n, state)
        state = d_n[..., None] * state + jnp.einsum('brhk,brhv->bhkv', k_n, v_n)
        return state, o_n

    s0 = jnp.zeros((bsz, HG_HEADS, HG_DK, HG_DV), f32)
    xs = (jnp.moveaxis(q_dec, 1, 0), jnp.moveaxis(k_end, 1, 0),
          jnp.moveaxis(vc, 1, 0), jnp.moveaxis(jnp.exp(blast), 1, 0))
    _, o_inter = lax.scan(step, s0, xs)
    o = o_intra + jnp.moveaxis(o_inter, 0, 1)
    o = o.reshape(bsz, lp, HG_HEADS, HG_DV)[:, pad:]
    o = o * lax.rsqrt(jnp.mean(o * o, axis=-1, keepdims=True) + EPS)
    o = o.reshape(bsz, length, HG_WIDTH) * norm_w.astype(f32)
    return (o * jax.nn.silu(g.astype(f32))).astype(q.dtype)


def mamba2_mixer(z, xbc, dt_pre, conv_w, conv_b, dt_bias, a_log, d_skip, norm_w):
    bsz, length, _ = z.shape
    f32 = jnp.float32
    xbc = jax.nn.silu(causal_dwconv(xbc, conv_w, conv_b).astype(f32))
    x_in = xbc[..., :M2_WIDTH].reshape(bsz, length, M2_GROUPS, M2_HPG, M2_HEADDIM)
    b_in = xbc[..., M2_WIDTH:M2_WIDTH + M2_GN].reshape(bsz, length, M2_GROUPS, M2_STATE)
    c_in = xbc[..., M2_WIDTH + M2_GN:].reshape(bsz, length, M2_GROUPS, M2_STATE)
    dt = jax.nn.softplus(dt_pre.astype(f32) + dt_bias.astype(f32))
    dt = dt.reshape(bsz, length, M2_GROUPS, M2_HPG)
    a = -jnp.exp(a_log.astype(f32)).reshape(M2_GROUPS, M2_HPG)
    da = dt * a
    xdt = x_in * dt[..., None]
    pad = (-N_META) % M2_CHUNK
    lp = length + pad
    xc = to_chunks(front_pad(xdt, pad), M2_CHUNK)
    ac = to_chunks(front_pad(da, pad), M2_CHUNK)
    bc = to_chunks(front_pad(b_in, pad), M2_CHUNK)
    cc = to_chunks(front_pad(c_in, pad), M2_CHUNK)
    a_cum = jnp.cumsum(ac, axis=2)
    causal = jnp.tril(jnp.ones((M2_CHUNK, M2_CHUNK), dtype=bool))
    seg = a_cum[:, :, :, None] - a_cum[:, :, None, :]
    l_mat = jnp.exp(jnp.where(causal[:, :, None, None], seg, -jnp.inf))
    cb = jnp.einsum('bnigs,bnjgs->bnijg', cc, bc)
    y_diag = jnp.einsum('bnijgh,bnjghp->bnighp', cb[..., None] * l_mat, xc)
    decay_to_end = jnp.exp(a_cum[:, :, -1:] - a_cum)
    u = jnp.einsum('bnjgs,bnjghp->bnghps', bc, xc * decay_to_end[..., None])
    chunk_decay = jnp.exp(a_cum[:, :, -1])

    def step(state, inp):
        c_n, e_n, u_n, d_n = inp
        y_n = jnp.einsum('bigs,bghps->bighp', c_n, state) * e_n[..., None]
        state = d_n[..., None, None] * state + u_n
        return state, y_n

    s0 = jnp.zeros((bsz, M2_GROUPS, M2_HPG, M2_HEADDIM, M2_STATE), f32)
    xs = (jnp.moveaxis(cc, 1, 0), jnp.moveaxis(jnp.exp(a_cum), 1, 0),
          jnp.moveaxis(u, 1, 0), jnp.moveaxis(chunk_decay, 1, 0))
    _, y_off = lax.scan(step, s0, xs)
    y = y_diag + jnp.moveaxis(y_off, 0, 1)
    y = y.reshape(bsz, lp, M2_GROUPS, M2_HPG, M2_HEADDIM)[:, pad:]
    y = y + d_skip.astype(f32).reshape(M2_GROUPS, M2_HPG, 1) * x_in
    y = y.reshape(bsz, length, M2_WIDTH) * jax.nn.silu(z.astype(f32))
    yg = y.reshape(bsz, length, M2_GROUPS, M2_WIDTH // M2_GROUPS)
    yg = yg * lax.rsqrt(jnp.mean(yg * yg, axis=-1, keepdims=True) + EPS)
    y = yg.reshape(bsz, length, M2_WIDTH) * norm_w.astype(f32)
    return y.astype(z.dtype)


def conv_glu_ffn(h, w_up, conv_w, conv_b, w_down):
    u = causal_dwconv(h @ w_up, conv_w, conv_b)
    gate, val = jnp.split(u, 2, axis=-1)
    return (jax.nn.silu(gate) * val) @ w_down


def setup_inputs(seed: int = 0) -> dict:
    key = jax.random.key(seed)
    ks = jax.random.split(key, 20)
    f32 = jnp.float32

    def nrm(k, shape, scale):
        return scale * jax.random.normal(k, shape, f32)

    def gain(k, shape):
        return 1.0 + 0.05 * jax.random.normal(k, shape, f32)

    dt0 = jnp.exp(jax.random.uniform(ks[8], (DEPTH, M2_HEADS), f32,
                                     minval=math.log(DT_MIN), maxval=math.log(DT_MAX)))
    dt_bias = dt0 + jnp.log(-jnp.expm1(-dt0))
    a_log = jnp.log(jax.random.uniform(ks[9], (DEPTH, M2_HEADS), f32, minval=1.0, maxval=16.0))
    return {
        'x': nrm(ks[0], (BATCH, SEQ, D_MODEL), 1.0),
        'meta_tokens': nrm(ks[1], (N_META, D_MODEL), 1.0),
        'norm1_w': gain(ks[2], (DEPTH, D_MODEL)),
        'w_in': nrm(ks[3], (DEPTH, D_MODEL, D_PROJ), D_MODEL ** -0.5),
        'hg_lb_logits': nrm(ks[4], (DEPTH + 1, HG_KEY), 0.1),
        'hg_norm_w': gain(ks[5], (DEPTH, HG_WIDTH)),
        'm2_conv_w': nrm(ks[6], (DEPTH, M2_CONV, M2_XBC), M2_CONV ** -0.5),
        'm2_conv_b': nrm(ks[7], (DEPTH, M2_XBC), 0.02),
        'm2_dt_bias': dt_bias,
        'm2_a_log': a_log,
        'm2_d': gain(ks[10], (DEPTH, M2_HEADS)),
        'm2_norm_w': gain(ks[11], (DEPTH, M2_WIDTH)),
        'w_out': nrm(ks[12], (DEPTH, D_MIX, D_MODEL), D_MIX ** -0.5),
        'norm2_w': gain(ks[13], (DEPTH, D_MODEL)),
        'ffn_w_up': nrm(ks[14], (DEPTH, D_MODEL, 2 * D_FF), D_MODEL ** -0.5),
        'ffn_conv_w': nrm(ks[15], (DEPTH, FFN_CONV, 2 * D_FF), FFN_CONV ** -0.5),
        'ffn_conv_b': nrm(ks[16], (DEPTH, 2 * D_FF), 0.02),
        'ffn_w_down': nrm(ks[17], (DEPTH, D_FF, D_MODEL), D_FF ** -0.5),
        'final_norm_w': gain(ks[18], (D_MODEL,)),
    }


def reference(x, meta_tokens, norm1_w, w_in, hg_lb_logits, hg_norm_w, m2_conv_w, m2_conv_b,
              m2_dt_bias, m2_a_log, m2_d, m2_norm_w, w_out, norm2_w, ffn_w_up, ffn_conv_w,
              ffn_conv_b, ffn_w_down, final_norm_w):
    bsz = x.shape[0]
    meta = jnp.broadcast_to(meta_tokens.astype(x.dtype)[None], (bsz, N_META, D_MODEL))
    h = jnp.concatenate([meta, x], axis=1)
    lbs = jnp.cumsum(jax.nn.softmax(hg_lb_logits.astype(jnp.float32), axis=0), axis=0)
    for l in range(DEPTH):
        u = rmsnorm(h, norm1_w[l])
        proj = u @ w_in[l]
        hg_q, hg_f, hg_i, hg_g, m2_z, m2_xbc, m2_dt = jnp.split(proj, PROJ_SPLITS, axis=-1)
        out_a = hgrn2_mixer(hg_q, hg_f, hg_i, hg_g, lbs[l], hg_norm_w[l])
        out_b = mamba2_mixer(m2_z, m2_xbc, m2_dt, m2_conv_w[l], m2_conv_b[l], m2_dt_bias[l],
                             m2_a_log[l], m2_d[l], m2_norm_w[l])
        h = h + jnp.concatenate([out_a, out_b], axis=-1) @ w_out[l]
        h = h + conv_glu_ffn(rmsnorm(h, norm2_w[l]), ffn_w_up[l], ffn_conv_w[l],
                             ffn_conv_b[l], ffn_w_down[l])
    y = rmsnorm(h, final_norm_w)
    return y[:, N_META:, :]
```

```python
import functools

import jax
import jax.numpy as jnp
from jax import lax
from jax.experimental import pallas as pl
from jax.experimental.pallas import tpu as pltpu

F32 = jnp.float32
BF16 = jnp.bfloat16

EPS = 1e-6
N_META = 16
LANES = 128
HALO = 8
HG_DK = 128
HG_DV = 128
HG_DIAG = 32
M2_HEADDIM = 64
M2_STATE = 128
M2_GROUPS = 2
M2_CONV = 4
FFN_CONV = 3
DT_PAD = 128
VMEM_LIMIT = 56 * 1024 * 1024


def _sigmoid(x):
    return 1.0 / (1.0 + jnp.exp(-x))


def _silu(x):
    return x * _sigmoid(x)


def _softplus(x):
    return jnp.maximum(x, 0.0) + jnp.log1p(jnp.exp(-jnp.abs(x)))


def _rmsnorm(x, w):
    ms = jnp.mean(x * x, axis=-1, keepdims=True)
    return x * lax.rsqrt(ms + EPS) * w


def _dot(a, b):
    return jnp.dot(a, b, preferred_element_type=F32)


def _dot_nt(a, b):
    return lax.dot_general(a, b, (((1,), (1,)), ((), ())), preferred_element_type=F32)


def _dot_tn(a, b):
    return lax.dot_general(a, b, (((0,), (0,)), ((), ())), preferred_element_type=F32)


def _split3(x):
    x1 = x.astype(BF16)
    r1 = x - x1.astype(F32)
    x2 = r1.astype(BF16)
    x3 = (r1 - x2.astype(F32)).astype(BF16)
    return x1, x2, x3


def _cumsum_rows(x):
    t = x.shape[0]
    r = lax.broadcasted_iota(jnp.int32, (t, t), 0)
    c = lax.broadcasted_iota(jnp.int32, (t, t), 1)
    tril = jnp.where(c <= r, 1.0, 0.0).astype(BF16)
    p1, p2, p3 = _split3(x)
    return _dot(tril, p1) + _dot(tril, p2) + _dot(tril, p3)


def _transpose_f32(x):
    r = lax.broadcasted_iota(jnp.int32, (LANES, LANES), 0)
    c = lax.broadcasted_iota(jnp.int32, (LANES, LANES), 1)
    eye = jnp.where(r == c, 1.0, 0.0).astype(BF16)
    p1, p2, p3 = _split3(x)
    return _dot_nt(eye, p1) + _dot_nt(eye, p2) + _dot_nt(eye, p3)


def _transpose_bf16(x):
    r = lax.broadcasted_iota(jnp.int32, (LANES, LANES), 0)
    c = lax.broadcasted_iota(jnp.int32, (LANES, LANES), 1)
    eye = jnp.where(r == c, 1.0, 0.0).astype(BF16)
    return _dot_nt(eye, x).astype(BF16)


def _block_ref_rows(b, block, row_in_block):
    t = b.shape[0]
    parts = [jnp.broadcast_to(b[a + row_in_block:a + row_in_block + 1], (block, b.shape[1]))
             for a in range(0, t, block)]
    return parts[0] if len(parts) == 1 else jnp.concatenate(parts, axis=0)


def _norm_proj_kernel(x_ref, nw_ref, w_ref, o_ref, *, col_chunk):
    u = _rmsnorm(x_ref[...], nw_ref[...]).astype(BF16)
    n = o_ref.shape[-1]
    for c0 in range(0, n, col_chunk):
        c1 = min(n, c0 + col_chunk)
        o_ref[:, c0:c1] = _dot(u, w_ref[:, c0:c1])


def _norm_proj(x, norm_w, w, *, tile):
    bsz, length, d = x.shape
    n = w.shape[1]
    return pl.pallas_call(
        functools.partial(_norm_proj_kernel, col_chunk=512),
        grid=(bsz, length // tile),
        in_specs=[
            pl.BlockSpec((None, tile, d), lambda b, t: (b, t, 0)),
            pl.BlockSpec((1, d), lambda b, t: (0, 0)),
            pl.BlockSpec((d, n), lambda b, t: (0, 0), pipeline_mode=pl.Buffered(1)),
        ],
        out_specs=pl.BlockSpec((None, tile, n), lambda b, t: (b, t, 0)),
        out_shape=jax.ShapeDtypeStruct((bsz, length, n), F32),
        compiler_params=pltpu.CompilerParams(
            dimension_semantics=("arbitrary", "arbitrary"), vmem_limit_bytes=VMEM_LIMIT),
        name="norm_proj",
    )(x, norm_w.reshape(1, d), w)


def _hgrn2_kernel(q_ref, f_ref, i_ref, g_ref, lbl_ref, nw_ref, s0_ref, o_ref, *rest, tile, emit_state):
    if emit_state:
        sfin_ref, s_scr = rest
    else:
        (s_scr,) = rest
    t_idx = pl.program_id(1)

    @pl.when(t_idx == 0)
    def _():
        s_scr[...] = s0_ref[...]

    n_heads = q_ref.shape[-1] // HG_DK
    logits = lbl_ref[...]
    e = jnp.exp(logits - jnp.max(logits, axis=0, keepdims=True))
    lb = e[0:1] / jnp.sum(e, axis=0, keepdims=True)
    f = lb + (1.0 - lb) * _sigmoid(f_ref[...])
    k_all = 1.0 - f
    b_all = _cumsum_rows(jnp.log(f))
    q_all = _silu(q_ref[...])

    row = lax.broadcasted_iota(jnp.int32, (tile, HG_DK), 0)
    r2 = lax.broadcasted_iota(jnp.int32, (tile, tile), 0)
    c2 = lax.broadcasted_iota(jnp.int32, (tile, tile), 1)
    diag = min(HG_DIAG, tile)
    levels = []
    n = tile
    while n > diag:
        levels.append(n)
        n //= 2
    second_half = {n: (row & (n - 1)) >= n // 2 for n in levels}
    same_block = {n: (r2 & -n) == (c2 & -n) for n in levels if n < tile}
    diag_mask = ((r2 & -diag) == (c2 & -diag)) & (c2 <= r2)

    for h in range(n_heads):
        sl = slice(h * HG_DK, (h + 1) * HG_DK)
        vsl = slice(h * HG_DV, (h + 1) * HG_DV)
        b = b_all[:, sl]
        q = q_all[:, sl]
        k = k_all[:, sl]
        v = i_ref[:, vsl].astype(BF16)

        scores = None
        for n in levels:
            ref = _block_ref_rows(b, n, n // 2 - 1)
            sec = second_half[n]
            x = jnp.exp(jnp.where(sec, b - ref, ref - b))
            qd = jnp.where(sec, q * x, 0.0).astype(BF16)
            kd = jnp.where(sec, 0.0, k * x).astype(BF16)
            s_n = _dot_nt(qd, kd)
            if n < tile:
                s_n = jnp.where(same_block[n], s_n, 0.0)
            scores = s_n if scores is None else scores + s_n
        ref = _block_ref_rows(b, diag, diag // 2 - 1)
        s_d = _dot_nt((q * jnp.exp(b - ref)).astype(BF16), (k * jnp.exp(ref - b)).astype(BF16))
        s_d = jnp.where(diag_mask, s_d, 0.0)
        scores = s_d if scores is None else scores + s_d

        state = s_scr[h]
        b_last = b[tile - 1:tile]
        q_dec = (q * jnp.exp(b)).astype(BF16)
        k_end = (k * jnp.exp(b_last - b)).astype(BF16)
        o = _dot(scores.astype(BF16), v) + _dot(q_dec, state.astype(BF16))
        decay = jnp.transpose(jnp.broadcast_to(jnp.exp(b_last), (HG_DK, HG_DK)))
        s_scr[h] = state * decay + _dot_tn(k_end, v)

        o = o * lax.rsqrt(jnp.mean(o * o, axis=-1, keepdims=True) + EPS)
        o_ref[:, vsl] = (o * nw_ref[:, vsl] * _silu(g_ref[:, vsl])).astype(o_ref.dtype)

    if emit_state:
        @pl.when(t_idx == pl.num_programs(1) - 1)
        def _():
            sfin_ref[...] = s_scr[...]


def _hgrn2(proj, lb_logits, norm_w, s0, *, tile, emit_state):
    bsz, length, _ = proj.shape
    key = lb_logits.shape[-1]
    width = norm_w.shape[-1]
    n_heads = key // HG_DK
    out_shape = [jax.ShapeDtypeStruct((bsz, length, width), BF16)]
    out_specs = [pl.BlockSpec((None, tile, width), lambda b, t: (b, t, 0))]
    if emit_state:
        out_shape.append(jax.ShapeDtypeStruct(s0.shape, F32))
        out_specs.append(pl.BlockSpec(s0.shape, lambda b, t: (0, 0, 0)))
    res = pl.pallas_call(
        functools.partial(_hgrn2_kernel, tile=tile, emit_state=emit_state),
        grid=(bsz, length // tile),
        in_specs=[
            pl.BlockSpec((None, tile, key), lambda b, t: (b, t, 0)),
            pl.BlockSpec((None, tile, key), lambda b, t: (b, t, 1)),
            pl.BlockSpec((None, tile, width), lambda b, t: (b, t, 2)),
            pl.BlockSpec((None, tile, width), lambda b, t: (b, t, 3)),
            pl.BlockSpec(lb_logits.shape, lambda b, t: (0, 0)),
            pl.BlockSpec((1, width), lambda b, t: (0, 0)),
            pl.BlockSpec(s0.shape, lambda b, t: (0, 0, 0)),
        ],
        out_specs=out_specs,
        out_shape=out_shape,
        scratch_shapes=[pltpu.VMEM((n_heads, HG_DK, HG_DV), F32)],
        compiler_params=pltpu.CompilerParams(
            dimension_semantics=("arbitrary", "arbitrary"), vmem_limit_bytes=VMEM_LIMIT),
        name="hgrn2",
    )(proj, proj, proj, proj, lb_logits, norm_w.reshape(1, width), s0)
    return res if emit_state else res[0]


def _mamba2_kernel(z_ref, x_ref, bc_ref, dt_ref, cw_ref, cb_ref, dtb_ref, alog_ref, d_ref, nw_ref,
                   s0_ref, halo0_ref, o_ref, *rest, tile, emit_state):
    if emit_state:
        sfin_ref, halofin_ref, s_scr, xext = rest
    else:
        s_scr, xext = rest
    t_idx = pl.program_id(1)
    width = x_ref.shape[-1]
    gn = bc_ref.shape[-1] // 2
    n_pairs = width // LANES
    pairs_per_group = n_pairs // M2_GROUPS

    @pl.when(t_idx == 0)
    def _():
        s_scr[...] = s0_ref[...]
        xext[0:HALO, :] = halo0_ref[...]

    xext[HALO:HALO + tile, 0:width] = x_ref[...]
    xext[HALO:HALO + tile, width:] = bc_ref[...]
    first = HALO - (M2_CONV - 1)
    acc = cb_ref[...] + xext[first:first + tile, :] * cw_ref[0:1, :]
    for kk in range(1, M2_CONV):
        acc = acc + xext[first + kk:first + kk + tile, :] * cw_ref[kk:kk + 1, :]
    xext[0:HALO, :] = xext[tile:tile + HALO, :]
    xbc = _silu(acc)
    x_in = xbc[:, 0:width]
    b_in = xbc[:, width:width + gn]
    c_in = xbc[:, width + gn:]

    dt = _softplus(dt_ref[...] + dtb_ref[...])
    da = dt * (-jnp.exp(alog_ref[...]))
    a_cum = _cumsum_rows(da)
    a_cum_t = _transpose_f32(a_cum)
    dt_t = _transpose_f32(dt)
    a_last = a_cum[tile - 1:tile]
    w_end_t = dt_t * jnp.exp(a_cum_t[:, tile - 1:tile] - a_cum_t)

    r2 = lax.broadcasted_iota(jnp.int32, (tile, tile), 0)
    c2 = lax.broadcasted_iota(jnp.int32, (tile, tile), 1)
    causal = c2 <= r2
    lane = lax.broadcasted_iota(jnp.int32, (1, LANES), 1)
    lo = lane < M2_HEADDIM

    y_pairs = []
    for g in range(M2_GROUPS):
        b_g = b_in[:, g * M2_STATE:(g + 1) * M2_STATE]
        c_g = c_in[:, g * M2_STATE:(g + 1) * M2_STATE]
        cb = _dot_nt(c_g.astype(BF16), b_g.astype(BF16))
        b_gt = _transpose_bf16(b_g.astype(BF16)).astype(F32)
        for p in range(pairs_per_group):
            pair = g * pairs_per_group + p
            xp = x_in[:, pair * LANES:(pair + 1) * LANES]
            state = s_scr[pair]
            y = None
            upd = None
            for half in range(2):
                h = 2 * pair + half
                keep = lo if half == 0 else jnp.logical_not(lo)
                xm = jnp.where(keep, xp, 0.0).astype(BF16)
                sm = jnp.where(keep, state, 0.0).astype(BF16)
                a_col = a_cum[:, h:h + 1]
                seg = a_col - a_cum_t[h:h + 1, :]
                l_mat = jnp.where(causal, jnp.exp(seg), 0.0)
                w = (cb * l_mat * dt_t[h:h + 1, :]).astype(BF16)
                c_s = (c_g * jnp.exp(a_col)).astype(BF16)
                y_h = _dot(w, xm) + _dot(c_s, sm)
                u_h = _dot((b_gt * w_end_t[h:h + 1, :]).astype(BF16), xm)
                y = y_h if y is None else y + y_h
                upd = u_h if upd is None else upd + u_h
            dec = jnp.where(lo, jnp.exp(a_last[:, 2 * pair:2 * pair + 1]),
                            jnp.exp(a_last[:, 2 * pair + 1:2 * pair + 2]))
            s_scr[pair] = state * dec + upd
            y_pairs.append(y)
    y = jnp.concatenate(y_pairs, axis=1)
    y = y + d_ref[...] * x_in
    y = y * _silu(z_ref[...])
    gw = width // M2_GROUPS
    for g in range(M2_GROUPS):
        yg = y[:, g * gw:(g + 1) * gw]
        yg = yg * lax.rsqrt(jnp.mean(yg * yg, axis=-1, keepdims=True) + EPS)
        o_ref[:, g * gw:(g + 1) * gw] = (yg * nw_ref[:, g * gw:(g + 1) * gw]).astype(o_ref.dtype)

    if emit_state:
        @pl.when(t_idx == pl.num_programs(1) - 1)
        def _():
            sfin_ref[...] = s_scr[...]
            halofin_ref[...] = xext[0:HALO, :]


def _mamba2(proj, col0, conv_w, conv_b, dt_bias, a_log, d_skip, norm_w, s0, halo0, *, tile, emit_state):
    bsz, length, _ = proj.shape
    width = norm_w.shape[-1]
    xbc_w = conv_w.shape[-1]
    bc_w = xbc_w - width
    n_pairs = width // LANES
    zb = col0 // width
    bcb = (col0 + 2 * width) // bc_w
    dtb = (col0 + 2 * width + bc_w) // DT_PAD
    assert col0 % width == 0 and (col0 + 2 * width) % bc_w == 0 and (col0 + 2 * width + bc_w) % DT_PAD == 0
    out_shape = [jax.ShapeDtypeStruct((bsz, length, width), BF16)]
    out_specs = [pl.BlockSpec((None, tile, width), lambda b, t: (b, t, 0))]
    if emit_state:
        out_shape += [jax.ShapeDtypeStruct(s0.shape, F32), jax.ShapeDtypeStruct(halo0.shape, F32)]
        out_specs += [pl.BlockSpec(s0.shape, lambda b, t: (0, 0, 0)),
                      pl.BlockSpec(halo0.shape, lambda b, t: (0, 0))]
    const2 = lambda b, t: (0, 0)
    res = pl.pallas_call(
        functools.partial(_mamba2_kernel, tile=tile, emit_state=emit_state),
        grid=(bsz, length // tile),
        in_specs=[
            pl.BlockSpec((None, tile, width), lambda b, t: (b, t, zb)),
            pl.BlockSpec((None, tile, width), lambda b, t: (b, t, zb + 1)),
            pl.BlockSpec((None, tile, bc_w), lambda b, t: (b, t, bcb)),
            pl.BlockSpec((None, tile, DT_PAD), lambda b, t: (b, t, dtb)),
            pl.BlockSpec(conv_w.shape, const2),
            pl.BlockSpec((1, xbc_w), const2),
            pl.BlockSpec((1, DT_PAD), const2),
            pl.BlockSpec((1, DT_PAD), const2),
            pl.BlockSpec((1, width), const2),
            pl.BlockSpec((1, width), const2),
            pl.BlockSpec(s0.shape, lambda b, t: (0, 0, 0)),
            pl.BlockSpec(halo0.shape, const2),
        ],
        out_specs=out_specs,
        out_shape=out_shape,
        scratch_shapes=[pltpu.VMEM((n_pairs, M2_STATE, LANES), F32),
                        pltpu.VMEM((tile + HALO, xbc_w), F32)],
        compiler_params=pltpu.CompilerParams(
            dimension_semantics=("arbitrary", "arbitrary"), vmem_limit_bytes=VMEM_LIMIT),
        name="mamba2",
    )(proj, proj, proj, proj, conv_w, conv_b.reshape(1, xbc_w), dt_bias, a_log,
      d_skip, norm_w.reshape(1, width), s0, halo0)
    return res if emit_state else res[0]


def _out_proj_kernel(h_ref, a_ref, b_ref, w_ref, o_ref):
    wa = a_ref.shape[-1]
    o_ref[...] = h_ref[...] + _dot(a_ref[...], w_ref[0:wa, :]) + _dot(b_ref[...], w_ref[wa:, :])


def _out_proj(h, out_a, out_b, w_out, *, tile):
    bsz, length, d = h.shape
    wa, wb = out_a.shape[-1], out_b.shape[-1]
    return pl.pallas_call(
        _out_proj_kernel,
        grid=(bsz, length // tile),
        in_specs=[
            pl.BlockSpec((None, tile, d), lambda b, t: (b, t, 0)),
            pl.BlockSpec((None, tile, wa), lambda b, t: (b, t, 0)),
            pl.BlockSpec((None, tile, wb), lambda b, t: (b, t, 0)),
            pl.BlockSpec(w_out.shape, lambda b, t: (0, 0), pipeline_mode=pl.Buffered(1)),
        ],
        out_specs=pl.BlockSpec((None, tile, d), lambda b, t: (b, t, 0)),
        out_shape=jax.ShapeDtypeStruct((bsz, length, d), F32),
        compiler_params=pltpu.CompilerParams(
            dimension_semantics=("arbitrary", "arbitrary"), vmem_limit_bytes=VMEM_LIMIT),
        name="out_proj",
    )(h, out_a, out_b, w_out)


def _ffn_kernel(h_ref, n2_ref, wup_ref, cw_ref, cb_ref, wdn_ref, fn_ref, halo0_ref, o_ref,
                uext, act, *, tile, col_chunk):
    t_idx = pl.program_id(1)
    d_ff = wdn_ref.shape[0]

    @pl.when(t_idx == 0)
    def _():
        uext[0:HALO, :] = halo0_ref[...]

    h = h_ref[...]
    u_in = _rmsnorm(h, n2_ref[...]).astype(BF16)
    first = HALO - (FFN_CONV - 1)

    def conv_cols(c0, c1):
        uext[HALO:HALO + tile, c0:c1] = _dot(u_in, wup_ref[:, c0:c1])
        acc = cb_ref[:, c0:c1] + uext[first:first + tile, c0:c1] * cw_ref[0:1, c0:c1]
        for kk in range(1, FFN_CONV):
            acc = acc + uext[first + kk:first + kk + tile, c0:c1] * cw_ref[kk:kk + 1, c0:c1]
        uext[0:HALO, c0:c1] = uext[tile:tile + HALO, c0:c1]
        return acc

    for c0 in range(0, d_ff, col_chunk):
        c1 = min(d_ff, c0 + col_chunk)
        gate = conv_cols(c0, c1)
        val = conv_cols(d_ff + c0, d_ff + c1)
        act[:, c0:c1] = (_silu(gate) * val).astype(BF16)

    h2 = h + _dot(act[...], wdn_ref[...])
    o_ref[...] = _rmsnorm(h2, fn_ref[...])


def _ffn(h, norm2_w, w_up, conv_w, conv_b, w_down, final_w, halo0, *, tile):
    bsz, length, d = h.shape
    d_ff = w_down.shape[0]
    const2 = lambda b, t: (0, 0)
    return pl.pallas_call(
        functools.partial(_ffn_kernel, tile=tile, col_chunk=256),
        grid=(bsz, length // tile),
        in_specs=[
            pl.BlockSpec((None, tile, d), lambda b, t: (b, t, 0)),
            pl.BlockSpec((1, d), const2),
            pl.BlockSpec(w_up.shape, const2, pipeline_mode=pl.Buffered(1)),
            pl.BlockSpec(conv_w.shape, const2),
            pl.BlockSpec((1, 2 * d_ff), const2),
            pl.BlockSpec(w_down.shape, const2, pipeline_mode=pl.Buffered(1)),
            pl.BlockSpec((1, d), const2),
            pl.BlockSpec(halo0.shape, const2),
        ],
        out_specs=pl.BlockSpec((None, tile, d), lambda b, t: (b, t, 0)),
        out_shape=jax.ShapeDtypeStruct((bsz, length, d), F32),
        scratch_shapes=[pltpu.VMEM((tile + HALO, 2 * d_ff), F32),
                        pltpu.VMEM((tile, d_ff), BF16)],
        compiler_params=pltpu.CompilerParams(
            dimension_semantics=("arbitrary", "arbitrary"), vmem_limit_bytes=VMEM_LIMIT),
        name="ffn",
    )(h, norm2_w.reshape(1, d), w_up, conv_w, conv_b.reshape(1, 2 * d_ff), w_down,
      final_w.reshape(1, d), halo0)


def _pick_tile(length, pref):
    t = pref
    while length % t:
        t //= 2
    return t


def kernel(x, meta_tokens, norm1_w, w_in, hg_lb_logits, hg_norm_w, m2_conv_w, m2_conv_b, m2_dt_bias,
           m2_a_log, m2_d, m2_norm_w, w_out, norm2_w, ffn_w_up, ffn_conv_w, ffn_conv_b, ffn_w_down,
           final_norm_w):
    assert norm1_w.shape[0] == 1, "one layer"
    bsz, seq, d = x.shape
    hg_key = hg_lb_logits.shape[-1]
    hg_width = hg_norm_w.shape[-1]
    m2_width = m2_norm_w.shape[-1]
    n_m2_heads = m2_dt_bias.shape[-1]
    xbc_w = m2_conv_w.shape[-1]
    d_ff = ffn_w_down.shape[1]
    n_hg_heads = hg_key // HG_DK
    col_m2 = 2 * hg_key + 2 * hg_width

    pad = DT_PAD - n_m2_heads
    w_in_p = jnp.pad(w_in[0], ((0, 0), (0, pad))).astype(BF16)
    dt_bias_p = jnp.pad(m2_dt_bias[0], (0, pad)).reshape(1, DT_PAD)
    a_log_p = jnp.pad(m2_a_log[0], (0, pad)).reshape(1, DT_PAD)
    d_skip_e = jnp.repeat(m2_d[0], M2_HEADDIM).reshape(1, m2_width)
    w_out_b = w_out[0].astype(BF16)
    w_up_b = ffn_w_up[0].astype(BF16)
    w_down_b = ffn_w_down[0].astype(BF16)

    def layer(h, hg_s0, m2_s0, m2_halo0, *, tile_dense, tile_mix, emit_state):
        proj = _norm_proj(h, norm1_w[0], w_in_p, tile=tile_dense)
        res_a = _hgrn2(proj, hg_lb_logits, hg_norm_w[0], hg_s0, tile=tile_mix, emit_state=emit_state)
        res_b = _mamba2(proj, col_m2, m2_conv_w[0], m2_conv_b[0], dt_bias_p, a_log_p, d_skip_e,
                        m2_norm_w[0], m2_s0, m2_halo0, tile=tile_mix, emit_state=emit_state)
        out_a = res_a[0] if emit_state else res_a
        out_b = res_b[0] if emit_state else res_b
        h1 = _out_proj(h, out_a, out_b, w_out_b, tile=tile_dense)
        return h1, res_a, res_b

    meta = meta_tokens.astype(x.dtype)[None]
    zeros_hg = jnp.zeros((n_hg_heads, HG_DK, HG_DV), F32)
    zeros_m2 = jnp.zeros((m2_width // LANES, M2_STATE, LANES), F32)
    zeros_halo = jnp.zeros((HALO, xbc_w), F32)
    h1_meta, res_a, res_b = layer(meta, zeros_hg, zeros_m2, zeros_halo,
                                  tile_dense=N_META, tile_mix=N_META, emit_state=True)
    hg_s0, m2_s0, m2_halo0 = res_a[1], res_b[1], res_b[2]
    u_meta = _norm_proj(h1_meta, norm2_w[0], w_up_b, tile=N_META)
    ffn_halo0 = u_meta[0, N_META - HALO:, :]

    tile_dense = _pick_tile(seq, 256)
    tile_mix = _pick_tile(seq, 128)
    h1, _, _ = layer(x, hg_s0, m2_s0, m2_halo0, tile_dense=tile_dense, tile_mix=tile_mix, emit_state=False)
    return _ffn(h1, norm2_w[0], w_up_b, ffn_conv_w[0], ffn_conv_b[0], w_down_b, final_norm_w,
                ffn_halo0, tile=tile_dense)
```

```python
import functools
import math

import jax
import jax.numpy as jnp
from jax import lax
from jax.experimental import pallas as pl
from jax.experimental.pallas import tpu as pltpu

F32 = jnp.float32
BF16 = jnp.bfloat16

EPS = 1e-6
N_META = 16
LANES = 128
SUBLANES = 8
HALO = SUBLANES
HG_DK = 128
HG_DV = 128
HG_DIAG = 32
M2_HEADDIM = 64
M2_STATE = 128
M2_GROUPS = 2
M2_CONV = 4
FFN_CONV = 3
DT_PAD = LANES
SUB_TILE = 128
DOT_COLS = 512
VMEM_LIMIT = 56 * 1024 * 1024
LOG2E = math.log2(math.e)
NEG_BIG = -1e30


def _silu(x):
    hx = 0.5 * x
    return hx + hx * jnp.tanh(hx)


def _softplus(x):
    return jnp.maximum(x, 0.0) + jnp.log1p(jnp.exp(-jnp.abs(x)))


def _rmsnorm(x, w):
    ms = jnp.mean(x * x, axis=-1, keepdims=True)
    return x * lax.rsqrt(ms + EPS) * w


def _dot(a, b):
    return jnp.dot(a, b, preferred_element_type=F32)


def _dot_cols(a, w_ref, c0, c1, chunk=DOT_COLS):
    parts = [_dot(a, w_ref[:, s:min(c1, s + chunk)]) for s in range(c0, c1, chunk)]
    return parts[0] if len(parts) == 1 else jnp.concatenate(parts, axis=1)


def _dot_nt(a, b):
    return lax.dot_general(a, b, (((1,), (1,)), ((), ())), preferred_element_type=F32)


def _dot_tn(a, b):
    return lax.dot_general(a, b, (((0,), (0,)), ((), ())), preferred_element_type=F32)


def _split3(x):
    x1 = x.astype(BF16)
    r1 = x - x1.astype(F32)
    x2 = r1.astype(BF16)
    x3 = (r1 - x2.astype(F32)).astype(BF16)
    return x1, x2, x3


def _cumsum_rows(x):
    t = x.shape[0]
    r = lax.broadcasted_iota(jnp.int32, (t, t), 0)
    c = lax.broadcasted_iota(jnp.int32, (t, t), 1)
    tril = jnp.where(c <= r, 1.0, 0.0).astype(BF16)
    p1, p2, p3 = _split3(x)
    return _dot(tril, p1) + _dot(tril, p2) + _dot(tril, p3)


def _eye_bf16():
    r = lax.broadcasted_iota(jnp.int32, (LANES, LANES), 0)
    c = lax.broadcasted_iota(jnp.int32, (LANES, LANES), 1)
    return jnp.where(r == c, 1.0, 0.0).astype(BF16)


def _transpose_f32(x):
    eye = _eye_bf16()
    p1, p2, p3 = _split3(x)
    return _dot_nt(eye, p1) + _dot_nt(eye, p2) + _dot_nt(eye, p3)


def _transpose_bf16(x):
    return _dot_nt(_eye_bf16(), x).astype(BF16)


def _block_ref_rows(b, block, row_in_block):
    t = b.shape[0]
    parts = [jnp.broadcast_to(b[a + row_in_block:a + row_in_block + 1], (block, b.shape[1]))
             for a in range(0, t, block)]
    return parts[0] if len(parts) == 1 else jnp.concatenate(parts, axis=0)


def _shift_rows(x, prev_row):
    rolled = pltpu.roll(x, 1, axis=0)
    first = lax.broadcasted_iota(jnp.int32, (SUBLANES, x.shape[1]), 0) == 0
    top = jnp.where(first, prev_row, rolled[0:SUBLANES])
    if x.shape[0] == SUBLANES:
        return top
    return jnp.concatenate([top, rolled[SUBLANES:]], axis=0)


def _causal_conv(x, halo, w_ref, b_ref, c0, c1, taps):
    acc = b_ref[:, c0:c1] + x * w_ref[taps - 1:taps, c0:c1]
    xs = x
    for j in range(1, taps):
        xs = _shift_rows(xs, halo[HALO - j:HALO - j + 1])
        acc = acc + xs * w_ref[taps - 1 - j:taps - j, c0:c1]
    return acc


def _norm_proj_kernel(x_ref, nw_ref, w_ref, o_ref, *, col_chunk):
    u = _rmsnorm(x_ref[...], nw_ref[...]).astype(BF16)
    n = o_ref.shape[-1]
    for c0 in range(0, n, col_chunk):
        c1 = min(n, c0 + col_chunk)
        o_ref[:, c0:c1] = _dot(u, w_ref[:, c0:c1])


def _norm_proj(x, norm_w, w, *, tile):
    bsz, length, d = x.shape
    n = w.shape[1]
    return pl.pallas_call(
        functools.partial(_norm_proj_kernel, col_chunk=512),
        grid=(bsz, length // tile),
        in_specs=[
            pl.BlockSpec((None, tile, d), lambda b, t: (b, t, 0)),
            pl.BlockSpec((1, d), lambda b, t: (0, 0)),
            pl.BlockSpec((d, n), lambda b, t: (0, 0), pipeline_mode=pl.Buffered(1)),
        ],
        out_specs=pl.BlockSpec((None, tile, n), lambda b, t: (b, t, 0)),
        out_shape=jax.ShapeDtypeStruct((bsz, length, n), F32),
        compiler_params=pltpu.CompilerParams(
            dimension_semantics=("arbitrary", "arbitrary"), vmem_limit_bytes=VMEM_LIMIT),
        name="norm_proj",
    )(x, norm_w.reshape(1, d), w)


def _hgrn2_scores(b, q, k, levels, second_half, level_mask, diag, diag_mask):
    ref = _block_ref_rows(b, diag, diag // 2 - 1)
    d = b - ref
    s_d = _dot_nt((q * jnp.exp2(d)).astype(BF16), (k * jnp.exp2(-d)).astype(BF16))
    scores = jnp.where(diag_mask, s_d, 0.0)
    for n in levels:
        ref = _block_ref_rows(b, n, n // 2 - 1)
        y = (jnp.exp2(-jnp.abs(b - ref)) * jnp.where(second_half[n], q, k)).astype(BF16)
        scores = jnp.where(level_mask[n], _dot_nt(y, y), scores)
    return scores


def _hgrn2_kernel(x_ref, n1_ref, w_ref, lbl_ref, nw_ref, wo_ref, s0_ref, o_ref, *rest,
                  tile, sub, emit_state):
    if emit_state:
        sfin_ref, s_scr, oa_scr, pr_scr = rest
    else:
        s_scr, oa_scr, pr_scr = rest
    t_idx = pl.program_id(1)

    @pl.when(t_idx == 0)
    def _():
        s_scr[...] = s0_ref[...]

    key = lbl_ref.shape[-1]
    pw = 2 * HG_DK
    n_pairs = key // pw

    x = x_ref[...]
    u = _rmsnorm(x, n1_ref[...]).astype(BF16)

    logits = lbl_ref[...]
    e = jnp.exp(logits - jnp.max(logits, axis=0, keepdims=True))
    lb = e[0:1] / jnp.sum(e, axis=0, keepdims=True)
    f_c0 = 0.5 * (1.0 + lb)
    f_c1 = 0.5 * (1.0 - lb)

    row = lax.broadcasted_iota(jnp.int32, (sub, HG_DK), 0)
    r2 = lax.broadcasted_iota(jnp.int32, (sub, sub), 0)
    c2 = lax.broadcasted_iota(jnp.int32, (sub, sub), 1)
    diag = min(HG_DIAG, sub)
    levels = []
    n = sub
    while n > diag:
        levels.append(n)
        n //= 2
    second_half = {n: (row & (n - 1)) >= n // 2 for n in levels}
    level_mask = {n: ((r2 & -n) == (c2 & -n)) & ((r2 & (n - 1)) >= n // 2) & ((c2 & (n - 1)) < n // 2)
                  for n in levels}
    diag_mask = ((r2 & -diag) == (c2 & -diag)) & (c2 <= r2)

    def project(p, j):
        pr_scr[p % 2, :, j * pw:(j + 1) * pw] = _dot(u, w_ref[:, (4 * p + j) * pw:(4 * p + j + 1) * pw])

    for j in range(4):
        project(0, j)
    n_steps = 2 * (tile // sub)
    for p in range(n_pairs):
        pr = pr_scr.at[p % 2]
        pending = [(p + 1, j) for j in range(4)] if p + 1 < n_pairs else []
        cs = slice(p * pw, (p + 1) * pw)
        f = f_c0[:, cs] + f_c1[:, cs] * jnp.tanh(0.5 * pr[:, pw:2 * pw])
        k2 = 1.0 - f
        lg2 = jnp.log2(f)
        q2 = _silu(pr[:, 0:pw])
        v2 = pr[:, 2 * pw:3 * pw].astype(BF16)
        gate2 = _silu(pr[:, 3 * pw:4 * pw])
        step = 0
        for st in range(tile // sub):
            rs = slice(st * sub, (st + 1) * sub)
            b2 = _cumsum_rows(lg2[rs])
            for hh in range(2):
                while pending and len(pending) * n_steps > 4 * (n_steps - 1 - step):
                    project(*pending.pop(0))
                step += 1
                head = 2 * p + hh
                hs = slice(hh * HG_DK, (hh + 1) * HG_DK)
                osl = slice(head * HG_DV, (head + 1) * HG_DV)
                b, q, k, v = b2[:, hs], q2[rs, hs], k2[rs, hs], v2[rs, hs]
                scores = _hgrn2_scores(b, q, k, levels, second_half, level_mask, diag, diag_mask)
                state = s_scr[head]
                b_last = b[sub - 1:sub]
                q_dec = (q * jnp.exp2(b)).astype(BF16)
                k_end = (k * jnp.exp2(b_last - b)).astype(BF16)
                o = _dot(scores.astype(BF16), v) + _dot(q_dec, state.astype(BF16))
                decay = jnp.transpose(jnp.broadcast_to(jnp.exp2(b_last), (HG_DK, HG_DK)))
                s_scr[head] = state * decay + _dot_tn(k_end, v)
                o = o * lax.rsqrt(jnp.mean(o * o, axis=-1, keepdims=True) + EPS)
                oa_scr[rs, osl] = (o * nw_ref[:, osl] * gate2[rs, hs]).astype(BF16)

    o_ref[...] = x + _dot_cols(oa_scr[...], wo_ref, 0, wo_ref.shape[-1])

    if emit_state:
        @pl.when(t_idx == pl.num_programs(1) - 1)
        def _():
            sfin_ref[...] = s_scr[...]


def _hgrn2(x, norm1_w, w_hg, lb_logits, norm_w, w_out_a, s0, *, tile, emit_state):
    bsz, length, d = x.shape
    width = norm_w.shape[-1]
    sub = min(SUB_TILE, tile)
    const2 = lambda b, t: (0, 0)
    const3 = lambda b, t: (0, 0, 0)
    out_shape = [jax.ShapeDtypeStruct((bsz, length, d), F32)]
    out_specs = [pl.BlockSpec((None, tile, d), lambda b, t: (b, t, 0))]
    if emit_state:
        out_shape.append(jax.ShapeDtypeStruct(s0.shape, F32))
        out_specs.append(pl.BlockSpec(s0.shape, const3))
    res = pl.pallas_call(
        functools.partial(_hgrn2_kernel, tile=tile, sub=sub, emit_state=emit_state),
        grid=(bsz, length // tile),
        in_specs=[
            pl.BlockSpec((None, tile, d), lambda b, t: (b, t, 0)),
            pl.BlockSpec((1, d), const2),
            pl.BlockSpec(w_hg.shape, const2, pipeline_mode=pl.Buffered(1)),
            pl.BlockSpec(lb_logits.shape, const2),
            pl.BlockSpec((1, width), const2),
            pl.BlockSpec(w_out_a.shape, const2, pipeline_mode=pl.Buffered(1)),
            pl.BlockSpec(s0.shape, const3),
        ],
        out_specs=out_specs,
        out_shape=out_shape,
        scratch_shapes=[pltpu.VMEM(s0.shape, F32), pltpu.VMEM((tile, width), BF16),
                        pltpu.VMEM((2, tile, 8 * HG_DK), F32)],
        compiler_params=pltpu.CompilerParams(
            dimension_semantics=("arbitrary", "arbitrary"), vmem_limit_bytes=VMEM_LIMIT),
        name="hgrn2",
    )(x, norm1_w.reshape(1, d), w_hg, lb_logits, norm_w.reshape(1, width), w_out_a, s0)
    return res if emit_state else res[0]


def _mamba2_kernel(x_ref, h_ref, n1_ref, w_ref, cw_ref, cb_ref, dtb_ref, alog_ref, d_ref, nw_ref, wo_ref,
                   s0_ref, halo0_ref, o_ref, *rest, tile, sub, emit_state):
    if emit_state:
        sfin_ref, halofin_ref, s_scr, halo_scr, ob_scr = rest
    else:
        s_scr, halo_scr, ob_scr = rest
    t_idx = pl.program_id(1)
    width = nw_ref.shape[-1]
    xbc_w = cw_ref.shape[-1]
    gn = (xbc_w - width) // 2
    n_pairs = width // LANES
    pairs_per_group = n_pairs // M2_GROUPS

    @pl.when(t_idx == 0)
    def _():
        s_scr[...] = s0_ref[...]
        halo_scr[...] = halo0_ref[...]

    u = _rmsnorm(x_ref[...], n1_ref[...]).astype(BF16)
    z = _dot_cols(u, w_ref, 0, width)
    xbc_raw = _dot_cols(u, w_ref, width, width + xbc_w)
    dt_raw = _dot_cols(u, w_ref, width + xbc_w, w_ref.shape[-1])

    xbc = _silu(_causal_conv(xbc_raw, halo_scr[...], cw_ref, cb_ref, 0, xbc_w, M2_CONV))
    halo_scr[...] = xbc_raw[tile - HALO:tile]
    x_all = xbc[:, 0:width]
    b_all = xbc[:, width:width + gn]
    c_all = xbc[:, width + gn:]

    r2 = lax.broadcasted_iota(jnp.int32, (sub, sub), 0)
    c2 = lax.broadcasted_iota(jnp.int32, (sub, sub), 1)
    causal_bias = jnp.where(c2 <= r2, 0.0, NEG_BIG)
    lane = lax.broadcasted_iota(jnp.int32, (1, LANES), 1)
    lo = lane < M2_HEADDIM
    a_log2 = -jnp.exp(alog_ref[...]) * LOG2E

    for st in range(tile // sub):
        rs = slice(st * sub, (st + 1) * sub)
        x_in, b_in, c_in = x_all[rs], b_all[rs], c_all[rs]
        dt = _softplus(dt_raw[rs] + dtb_ref[...])
        a_cum = _cumsum_rows(dt * a_log2)
        a_cum_t = _transpose_f32(a_cum)
        dt_t = _transpose_f32(dt)
        a_last = a_cum[sub - 1:sub]
        a_src_t = a_cum_t - jnp.log2(dt_t)
        w_end_t = dt_t * jnp.exp2(a_cum_t[:, sub - 1:sub] - a_cum_t)
        e_cum = jnp.exp2(a_cum)

        y_pairs = []
        for g in range(M2_GROUPS):
            b_g = b_in[:, g * M2_STATE:(g + 1) * M2_STATE]
            c_g = c_in[:, g * M2_STATE:(g + 1) * M2_STATE]
            cb = _dot_nt(c_g.astype(BF16), b_g.astype(BF16))
            b_gt = _transpose_bf16(b_g.astype(BF16)).astype(F32)
            for p in range(pairs_per_group):
                pair = g * pairs_per_group + p
                xp = x_in[:, pair * LANES:(pair + 1) * LANES]
                state = s_scr[pair]
                y = None
                upd = None
                for half in range(2):
                    h = 2 * pair + half
                    keep = lo if half == 0 else jnp.logical_not(lo)
                    xm = jnp.where(keep, xp, 0.0).astype(BF16)
                    sm = jnp.where(keep, state, 0.0).astype(BF16)
                    seg = (a_cum[:, h:h + 1] + causal_bias) - a_src_t[h:h + 1, :]
                    w = (cb * jnp.exp2(seg)).astype(BF16)
                    c_s = (c_g * e_cum[:, h:h + 1]).astype(BF16)
                    y_h = _dot(w, xm) + _dot(c_s, sm)
                    u_h = _dot((b_gt * w_end_t[h:h + 1, :]).astype(BF16), xm)
                    y = y_h if y is None else y + y_h
                    upd = u_h if upd is None else upd + u_h
                dec = jnp.where(lo, jnp.exp2(a_last[:, 2 * pair:2 * pair + 1]),
                                jnp.exp2(a_last[:, 2 * pair + 1:2 * pair + 2]))
                s_scr[pair] = state * dec + upd
                y_pairs.append(y)
        y = jnp.concatenate(y_pairs, axis=1)
        y = (y + d_ref[...] * x_in) * _silu(z[rs])
        gw = width // M2_GROUPS
        for g in range(M2_GROUPS):
            yg = y[:, g * gw:(g + 1) * gw]
            yg = yg * lax.rsqrt(jnp.mean(yg * yg, axis=-1, keepdims=True) + EPS)
            ob_scr[rs, g * gw:(g + 1) * gw] = (yg * nw_ref[:, g * gw:(g + 1) * gw]).astype(BF16)

    o_ref[...] = h_ref[...] + _dot_cols(ob_scr[...], wo_ref, 0, wo_ref.shape[-1])

    if emit_state:
        @pl.when(t_idx == pl.num_programs(1) - 1)
        def _():
            sfin_ref[...] = s_scr[...]
            halofin_ref[...] = halo_scr[...]


def _mamba2(x, h, norm1_w, w_m2, conv_w, conv_b, dt_bias, a_log, d_skip, norm_w, w_out_b, s0, halo0,
            *, tile, emit_state):
    bsz, length, d = x.shape
    width = norm_w.shape[-1]
    xbc_w = conv_w.shape[-1]
    sub = min(SUB_TILE, tile)
    const2 = lambda b, t: (0, 0)
    const3 = lambda b, t: (0, 0, 0)
    row_spec = pl.BlockSpec((None, tile, d), lambda b, t: (b, t, 0))
    out_shape = [jax.ShapeDtypeStruct((bsz, length, d), F32)]
    out_specs = [row_spec]
    if emit_state:
        out_shape += [jax.ShapeDtypeStruct(s0.shape, F32), jax.ShapeDtypeStruct(halo0.shape, F32)]
        out_specs += [pl.BlockSpec(s0.shape, const3), pl.BlockSpec(halo0.shape, const2)]
    res = pl.pallas_call(
        functools.partial(_mamba2_kernel, tile=tile, sub=sub, emit_state=emit_state),
        grid=(bsz, length // tile),
        in_specs=[
            row_spec,
            row_spec,
            pl.BlockSpec((1, d), const2),
            pl.BlockSpec(w_m2.shape, const2, pipeline_mode=pl.Buffered(1)),
            pl.BlockSpec(conv_w.shape, const2),
            pl.BlockSpec((1, xbc_w), const2),
            pl.BlockSpec((1, DT_PAD), const2),
            pl.BlockSpec((1, DT_PAD), const2),
            pl.BlockSpec((1, width), const2),
            pl.BlockSpec((1, width), const2),
            pl.BlockSpec(w_out_b.shape, const2, pipeline_mode=pl.Buffered(1)),
            pl.BlockSpec(s0.shape, const3),
            pl.BlockSpec(halo0.shape, const2),
        ],
        out_specs=out_specs,
        out_shape=out_shape,
        scratch_shapes=[pltpu.VMEM(s0.shape, F32), pltpu.VMEM(halo0.shape, F32),
                        pltpu.VMEM((tile, width), BF16)],
        compiler_params=pltpu.CompilerParams(
            dimension_semantics=("arbitrary", "arbitrary"), vmem_limit_bytes=VMEM_LIMIT),
        name="mamba2",
    )(x, h, norm1_w.reshape(1, d), w_m2, conv_w, conv_b.reshape(1, xbc_w), dt_bias, a_log,
      d_skip, norm_w.reshape(1, width), w_out_b, s0, halo0)
    return res if emit_state else res[0]


def _ffn_kernel(h_ref, n2_ref, wup_ref, cw_ref, cb_ref, wdn_ref, fn_ref, halo0_ref, o_ref,
                halo_scr, act, *, tile, col_chunk):
    t_idx = pl.program_id(1)
    d_ff = wdn_ref.shape[0]

    @pl.when(t_idx == 0)
    def _():
        halo_scr[...] = halo0_ref[...]

    h = h_ref[...]
    u_in = _rmsnorm(h, n2_ref[...]).astype(BF16)

    def conv_cols(c0, c1):
        raw = _dot(u_in, wup_ref[:, c0:c1])
        out = _causal_conv(raw, halo_scr[:, c0:c1], cw_ref, cb_ref, c0, c1, FFN_CONV)
        halo_scr[:, c0:c1] = raw[tile - HALO:tile]
        return out

    for c0 in range(0, d_ff, col_chunk):
        c1 = min(d_ff, c0 + col_chunk)
        gate = conv_cols(c0, c1)
        val = conv_cols(d_ff + c0, d_ff + c1)
        act[:, c0:c1] = (_silu(gate) * val).astype(BF16)

    h2 = h + _dot_cols(act[...], wdn_ref, 0, wdn_ref.shape[-1])
    o_ref[...] = _rmsnorm(h2, fn_ref[...])


def _ffn(h, norm2_w, w_up, conv_w, conv_b, w_down, final_w, halo0, *, tile):
    bsz, length, d = h.shape
    d_ff = w_down.shape[0]
    const2 = lambda b, t: (0, 0)
    return pl.pallas_call(
        functools.partial(_ffn_kernel, tile=tile, col_chunk=256),
        grid=(bsz, length // tile),
        in_specs=[
            pl.BlockSpec((None, tile, d), lambda b, t: (b, t, 0)),
            pl.BlockSpec((1, d), const2),
            pl.BlockSpec(w_up.shape, const2, pipeline_mode=pl.Buffered(1)),
            pl.BlockSpec(conv_w.shape, const2),
            pl.BlockSpec((1, 2 * d_ff), const2),
            pl.BlockSpec(w_down.shape, const2, pipeline_mode=pl.Buffered(1)),
            pl.BlockSpec((1, d), const2),
            pl.BlockSpec(halo0.shape, const2),
        ],
        out_specs=pl.BlockSpec((None, tile, d), lambda b, t: (b, t, 0)),
        out_shape=jax.ShapeDtypeStruct((bsz, length, d), F32),
        scratch_shapes=[pltpu.VMEM((HALO, 2 * d_ff), F32),
                        pltpu.VMEM((tile, d_ff), BF16)],
        compiler_params=pltpu.CompilerParams(
            dimension_semantics=("arbitrary", "arbitrary"), vmem_limit_bytes=VMEM_LIMIT),
        name="ffn",
    )(h, norm2_w.reshape(1, d), w_up, conv_w, conv_b.reshape(1, 2 * d_ff), w_down,
      final_w.reshape(1, d), halo0)


def _pick_tile(length, pref):
    t = pref
    while length % t:
        t //= 2
    return t


def kernel(x, meta_tokens, norm1_w, w_in, hg_lb_logits, hg_norm_w, m2_conv_w, m2_conv_b, m2_dt_bias,
           m2_a_log, m2_d, m2_norm_w, w_out, norm2_w, ffn_w_up, ffn_conv_w, ffn_conv_b, ffn_w_down,
           final_norm_w):
    assert norm1_w.shape[0] == 1, "one layer"
    bsz, seq, d = x.shape
    hg_key = hg_lb_logits.shape[-1]
    hg_width = hg_norm_w.shape[-1]
    m2_width = m2_norm_w.shape[-1]
    n_m2_heads = m2_dt_bias.shape[-1]
    xbc_w = m2_conv_w.shape[-1]
    n_hg_heads = hg_key // HG_DK
    assert hg_key == hg_width, "q/f and i/g column groups share one pair layout"
    col_m2 = 2 * hg_key + 2 * hg_width

    pw = 2 * HG_DK
    n_pairs = hg_key // pw
    w_hg = w_in[0][:, :col_m2].reshape(d, 4, n_pairs, pw).transpose(0, 2, 1, 3).reshape(d, col_m2).astype(BF16)
    pad = DT_PAD - n_m2_heads
    w_m2 = jnp.pad(w_in[0][:, col_m2:], ((0, 0), (0, pad))).astype(BF16)
    dt_bias_p = jnp.pad(m2_dt_bias[0], (0, pad)).reshape(1, DT_PAD)
    a_log_p = jnp.pad(m2_a_log[0], (0, pad)).reshape(1, DT_PAD)
    d_skip_e = jnp.repeat(m2_d[0], M2_HEADDIM).reshape(1, m2_width)
    w_out_a = w_out[0][:hg_width].astype(BF16)
    w_out_b = w_out[0][hg_width:].astype(BF16)
    w_up_b = ffn_w_up[0].astype(BF16)
    w_down_b = ffn_w_down[0].astype(BF16)

    def mixers(h, hg_s0, m2_s0, m2_halo0, *, tile, emit_state):
        res_a = _hgrn2(h, norm1_w[0], w_hg, hg_lb_logits, hg_norm_w[0], w_out_a, hg_s0,
                       tile=tile, emit_state=emit_state)
        h_a = res_a[0] if emit_state else res_a
        res_b = _mamba2(h, h_a, norm1_w[0], w_m2, m2_conv_w[0], m2_conv_b[0], dt_bias_p, a_log_p, d_skip_e,
                        m2_norm_w[0], w_out_b, m2_s0, m2_halo0, tile=tile, emit_state=emit_state)
        return res_a, res_b

    meta = meta_tokens.astype(x.dtype)[None]
    zeros_hg = jnp.zeros((n_hg_heads, HG_DK, HG_DV), F32)
    zeros_m2 = jnp.zeros((m2_width // LANES, M2_STATE, LANES), F32)
    zeros_halo = jnp.zeros((HALO, xbc_w), F32)
    res_a, res_b = mixers(meta, zeros_hg, zeros_m2, zeros_halo, tile=N_META, emit_state=True)
    hg_s0, (h1_meta, m2_s0, m2_halo0) = res_a[1], res_b
    u_meta = _norm_proj(h1_meta, norm2_w[0], w_up_b, tile=N_META)
    ffn_halo0 = u_meta[0, N_META - HALO:, :]

    tile = _pick_tile(seq, 256)
    _, h1 = mixers(x, hg_s0, m2_s0, m2_halo0, tile=tile, emit_state=False)
    return _ffn(h1, norm2_w[0], w_up_b, ffn_conv_w[0], ffn_conv_b[0], w_down_b, final_norm_w,
                ffn_halo0, tile=tile)
```

```python
import functools
import math

import jax
import jax.numpy as jnp
from jax import lax
from jax.experimental import pallas as pl
from jax.experimental.pallas import tpu as pltpu

F32 = jnp.float32
BF16 = jnp.bfloat16

EPS = 1e-6
N_META = 16
LANES = 128
SUBLANES = 8
HALO = SUBLANES
HG_DK = 128
HG_DV = 128
HG_DIAG = 32
M2_HEADDIM = 64
M2_STATE = 128
M2_GROUPS = 2
M2_CONV = 4
FFN_CONV = 3
DT_PAD = LANES
SUB_TILE = 128
MIXER_TILE = 256
FFN_TILE = 512
PIECE_COLS = 256
VMEM_LIMIT = 56 * 1024 * 1024
LOG2E = math.log2(math.e)
NEG_BIG = -1e30


def _silu(x):
    hx = 0.5 * x
    return hx + hx * jnp.tanh(hx)


def _softplus(x):
    return jnp.maximum(x, 0.0) + jnp.log1p(jnp.exp(-jnp.abs(x)))


def _rmsnorm(x, w):
    ms = jnp.mean(x * x, axis=-1, keepdims=True)
    return x * lax.rsqrt(ms + EPS) * w


def _dot(a, b):
    return jnp.dot(a, b, preferred_element_type=F32)


def _dot_nt(a, b):
    return lax.dot_general(a, b, (((1,), (1,)), ((), ())), preferred_element_type=F32)


def _dot_tn(a, b):
    return lax.dot_general(a, b, (((0,), (0,)), ((), ())), preferred_element_type=F32)


def _split3(x):
    x1 = x.astype(BF16)
    r1 = x - x1.astype(F32)
    x2 = r1.astype(BF16)
    x3 = (r1 - x2.astype(F32)).astype(BF16)
    return x1, x2, x3


def _cumsum_rows(x):
    t = x.shape[0]
    r = lax.broadcasted_iota(jnp.int32, (t, t), 0)
    c = lax.broadcasted_iota(jnp.int32, (t, t), 1)
    tril = jnp.where(c <= r, 1.0, 0.0).astype(BF16)
    p1, p2, p3 = _split3(x)
    return _dot(tril, p1) + _dot(tril, p2) + _dot(tril, p3)


def _eye_bf16():
    r = lax.broadcasted_iota(jnp.int32, (LANES, LANES), 0)
    c = lax.broadcasted_iota(jnp.int32, (LANES, LANES), 1)
    return jnp.where(r == c, 1.0, 0.0).astype(BF16)


def _transpose_f32(x):
    if x.shape[0] == x.shape[1]:
        return jnp.transpose(x)
    eye = _eye_bf16()
    p1, p2, p3 = _split3(x)
    return _dot_nt(eye, p1) + _dot_nt(eye, p2) + _dot_nt(eye, p3)


def _block_ref_rows(b, block, row_in_block):
    t = b.shape[0]
    parts = [jnp.broadcast_to(b[a + row_in_block:a + row_in_block + 1], (block, b.shape[1]))
             for a in range(0, t, block)]
    return parts[0] if len(parts) == 1 else jnp.concatenate(parts, axis=0)


def _shift_rows(x, prev_row):
    rolled = pltpu.roll(x, 1, axis=0)
    first = lax.broadcasted_iota(jnp.int32, (SUBLANES, x.shape[1]), 0) == 0
    top = jnp.where(first, prev_row, rolled[0:SUBLANES])
    if x.shape[0] == SUBLANES:
        return top
    return jnp.concatenate([top, rolled[SUBLANES:]], axis=0)


def _causal_conv(x, halo, w_ref, b_ref, c0, c1, taps):
    acc = b_ref[:, c0:c1] + x * w_ref[taps - 1:taps, c0:c1]
    xs = x
    for j in range(1, taps):
        xs = _shift_rows(xs, halo[HALO - j:HALO - j + 1])
        acc = acc + xs * w_ref[taps - 1 - j:taps - j, c0:c1]
    return acc


def _spread(n_items, n_slots):
    return [(n_items * (i + 1) + n_slots - 1) // n_slots - (n_items * i + n_slots - 1) // n_slots
            for i in range(n_slots)]


def _norm_proj_kernel(x_ref, nw_ref, w_ref, o_ref, *, col_chunk):
    u = _rmsnorm(x_ref[...], nw_ref[...]).astype(BF16)
    n = o_ref.shape[-1]
    for c0 in range(0, n, col_chunk):
        c1 = min(n, c0 + col_chunk)
        o_ref[:, c0:c1] = _dot(u, w_ref[:, c0:c1])


def _norm_proj(x, norm_w, w, *, tile):
    bsz, length, d = x.shape
    n = w.shape[1]
    return pl.pallas_call(
        functools.partial(_norm_proj_kernel, col_chunk=512),
        grid=(bsz, length // tile),
        in_specs=[
            pl.BlockSpec((None, tile, d), lambda b, t: (b, t, 0)),
            pl.BlockSpec((1, d), lambda b, t: (0, 0)),
            pl.BlockSpec((d, n), lambda b, t: (0, 0), pipeline_mode=pl.Buffered(1)),
        ],
        out_specs=pl.BlockSpec((None, tile, n), lambda b, t: (b, t, 0)),
        out_shape=jax.ShapeDtypeStruct((bsz, length, n), F32),
        compiler_params=pltpu.CompilerParams(
            dimension_semantics=("arbitrary", "arbitrary"), vmem_limit_bytes=VMEM_LIMIT),
        name="norm_proj",
    )(x, norm_w.reshape(1, d), w)


def _hgrn2_scores(b, q, k, levels, second_half, level_mask, diag, diag_mask):
    ref = _block_ref_rows(b, diag, diag // 2 - 1)
    d = b - ref
    s_d = _dot_nt((q * jnp.exp2(d)).astype(BF16), (k * jnp.exp2(-d)).astype(BF16))
    scores = jnp.where(diag_mask, s_d, 0.0)
    for n in levels:
        ref = _block_ref_rows(b, n, n // 2 - 1)
        y = (jnp.exp2(-jnp.abs(b - ref)) * jnp.where(second_half[n], q, k)).astype(BF16)
        scores = jnp.where(level_mask[n], _dot_nt(y, y), scores)
    return scores


def _mixer_kernel(x_ref, n1_ref, w_ref, lbl_ref, hnw_ref, cw_ref, cb_ref, dtb_ref, alog_ref, d_ref, mnw_ref,
                  wo_ref, hs0_ref, ms0_ref, halo0_ref, o_ref, *rest, tile, sub, emit_state):
    if emit_state:
        hsfin_ref, msfin_ref, halofin_ref, hs_scr, ms_scr, halo_scr, mix_scr, pr_scr, mp_scr = rest
    else:
        hs_scr, ms_scr, halo_scr, mix_scr, pr_scr, mp_scr = rest
    t_idx = pl.program_id(1)

    @pl.when(t_idx == 0)
    def _():
        hs_scr[...] = hs0_ref[...]
        ms_scr[...] = ms0_ref[...]
        halo_scr[...] = halo0_ref[...]

    key = lbl_ref.shape[-1]
    pw = 2 * HG_DK
    n_hg_pairs = key // pw
    width = mnw_ref.shape[-1]
    xbc_w = cw_ref.shape[-1]
    gn = (xbc_w - width) // 2
    n_m2_pairs = width // LANES
    pairs_per_group = n_m2_pairs // M2_GROUPS
    col_m2 = 4 * key
    n_sub = tile // sub
    stacked = sub % LANES == 0

    x = x_ref[...]
    u = _rmsnorm(x, n1_ref[...]).astype(BF16)

    def hg_piece(p, j):
        c0 = j * key + p * pw
        pr_scr[p % 2, :, j * pw:(j + 1) * pw] = _dot(u, w_ref[:, c0:c0 + pw])

    def m2_piece(c0, c1):
        mp_scr[:, c0:c1] = _dot(u, w_ref[:, col_m2 + c0:col_m2 + c1])

    m2_cols = w_ref.shape[-1] - col_m2
    xbc_pieces = [functools.partial(m2_piece, c0, min(c0 + PIECE_COLS, m2_cols))
                  for c0 in range(width, m2_cols, PIECE_COLS)]
    z_pieces = [functools.partial(m2_piece, c0, c0 + PIECE_COLS) for c0 in range(0, width, PIECE_COLS)]

    def out_piece_a(c0, c1):
        o_ref[:, c0:c1] = x[:, c0:c1] + _dot(mix_scr[:, 0:key], wo_ref[0:key, c0:c1])

    def out_piece_b(rs, c0, c1):
        o_ref[rs, c0:c1] = o_ref[rs, c0:c1] + _dot(mix_scr[rs, key:], wo_ref[key:, c0:c1])

    d_model = o_ref.shape[-1]
    out_cols = [(c0, c0 + PIECE_COLS) for c0 in range(0, d_model, PIECE_COLS)]

    logits = lbl_ref[...]
    e = jnp.exp(logits - jnp.max(logits, axis=0, keepdims=True))
    lb = e[0:1] / jnp.sum(e, axis=0, keepdims=True)
    f_c0 = 0.5 * (1.0 + lb)
    f_c1 = 0.5 * (1.0 - lb)

    row = lax.broadcasted_iota(jnp.int32, (sub, HG_DK), 0)
    r2 = lax.broadcasted_iota(jnp.int32, (sub, sub), 0)
    c2 = lax.broadcasted_iota(jnp.int32, (sub, sub), 1)
    diag = min(HG_DIAG, sub)
    levels = []
    n = sub
    while n > diag:
        levels.append(n)
        n //= 2
    second_half = {n: (row & (n - 1)) >= n // 2 for n in levels}
    level_mask = {n: ((r2 & -n) == (c2 & -n)) & ((r2 & (n - 1)) >= n // 2) & ((c2 & (n - 1)) < n // 2)
                  for n in levels}
    diag_mask = ((r2 & -diag) == (c2 & -diag)) & (c2 <= r2)

    def hg_head_step(pr, p, st, hh, b2, q2, k2, v2, gate2):
        rs = slice(st * sub, (st + 1) * sub)
        head = 2 * p + hh
        hs = slice(hh * HG_DK, (hh + 1) * HG_DK)
        osl = slice(head * HG_DV, (head + 1) * HG_DV)
        b, q, k, v = b2[:, hs], q2[rs, hs], k2[rs, hs], v2[rs, hs]
        scores = _hgrn2_scores(b, q, k, levels, second_half, level_mask, diag, diag_mask).astype(BF16)
        state = hs_scr[head]
        b_last = b[sub - 1:sub]
        q_dec = (q * jnp.exp2(b)).astype(BF16)
        k_end = (k * jnp.exp2(b_last - b)).astype(BF16)
        if stacked:
            o = _dot(jnp.concatenate([scores, q_dec], axis=1),
                     jnp.concatenate([v, state.astype(BF16)], axis=0))
        else:
            o = _dot(scores, v) + _dot(q_dec, state.astype(BF16))
        decay = jnp.transpose(jnp.broadcast_to(jnp.exp2(b_last), (HG_DK, HG_DK)))
        hs_scr[head] = state * decay + _dot_tn(k_end, v)
        o = o * lax.rsqrt(jnp.mean(o * o, axis=-1, keepdims=True) + EPS)
        mix_scr[rs, osl] = (o * hnw_ref[:, osl] * gate2[rs, hs]).astype(BF16)

    for j in range(4):
        hg_piece(0, j)
    m2_queue = xbc_pieces + z_pieces
    m2_share = _spread(len(m2_queue), n_hg_pairs)
    for p in range(n_hg_pairs):
        queue = [functools.partial(hg_piece, p + 1, j) for j in range(4)] if p + 1 < n_hg_pairs else []
        for _ in range(m2_share[p]):
            queue.append(m2_queue.pop(0))
        counts = _spread(len(queue), 2 * n_sub)
        pr = pr_scr.at[p % 2]
        cs = slice(p * pw, (p + 1) * pw)
        f = f_c0[:, cs] + f_c1[:, cs] * jnp.tanh(0.5 * pr[:, pw:2 * pw])
        k2 = 1.0 - f
        lg2 = jnp.log2(f)
        q2 = _silu(pr[:, 0:pw])
        v2 = pr[:, 2 * pw:3 * pw].astype(BF16)
        gate2 = _silu(pr[:, 3 * pw:4 * pw])
        step = 0
        b_tile = _cumsum_rows(lg2)
        for st in range(n_sub):
            b2 = b_tile[st * sub:(st + 1) * sub]
            if st:
                b2 = b2 - b_tile[st * sub - 1:st * sub]
            for hh in range(2):
                for _ in range(counts[step]):
                    queue.pop(0)()
                step += 1
                hg_head_step(pr, p, st, hh, b2, q2, k2, v2, gate2)

    conv_chunk = 2 * PIECE_COLS
    conv_starts = list(range(0, xbc_w, conv_chunk))
    out_share = _spread(len(out_cols), len(conv_starts))
    out_queue = [functools.partial(out_piece_a, c0, c1) for c0, c1 in out_cols]
    xbc_parts = []
    for i, c0 in enumerate(conv_starts):
        for _ in range(out_share[i]):
            out_queue.pop(0)()
        c1 = min(xbc_w, c0 + conv_chunk)
        raw = mp_scr[:, width + c0:width + c1]
        xbc_parts.append(_silu(_causal_conv(raw, halo_scr[:, c0:c1], cw_ref, cb_ref, c0, c1, M2_CONV)))
        halo_scr[:, c0:c1] = raw[tile - HALO:tile]
    xbc = jnp.concatenate(xbc_parts, axis=1)
    x_all = xbc[:, 0:width]
    b_all = xbc[:, width:width + gn]
    c_all = xbc[:, width + gn:]
    dt_raw = mp_scr[:, width + xbc_w:]

    causal_bias = jnp.where(c2 <= r2, 0.0, NEG_BIG)
    lane = lax.broadcasted_iota(jnp.int32, (1, LANES), 1)
    lo = lane < M2_HEADDIM
    a_log2 = -jnp.exp(alog_ref[...]) * LOG2E

    def m2_pair_step(pair, c_g, cb, b_gt, x_in, a_cum, a_src_t, w_end_t, e_cum, a_last):
        xp = x_in[:, pair * LANES:(pair + 1) * LANES]
        state = ms_scr[pair]
        lhs, rhs, upd_l, upd_r = [], [], [], []
        for half in range(2):
            h = 2 * pair + half
            keep = lo if half == 0 else jnp.logical_not(lo)
            xm = jnp.where(keep, xp, 0.0).astype(BF16)
            sm = jnp.where(keep, state, 0.0).astype(BF16)
            seg = (a_cum[:, h:h + 1] + causal_bias) - a_src_t[h:h + 1, :]
            lhs += [(cb * jnp.exp2(seg)).astype(BF16), (c_g * e_cum[:, h:h + 1]).astype(BF16)]
            rhs += [xm, sm]
            upd_l.append((b_gt * w_end_t[h:h + 1, :]).astype(BF16))
            upd_r.append(xm)
        if stacked:
            y = _dot(jnp.concatenate(lhs, axis=1), jnp.concatenate(rhs, axis=0))
            upd = _dot(jnp.concatenate(upd_l, axis=1), jnp.concatenate(upd_r, axis=0))
        else:
            y = sum(_dot(a, b) for a, b in zip(lhs, rhs))
            upd = sum(_dot(a, b) for a, b in zip(upd_l, upd_r))
        dec = jnp.where(lo, jnp.exp2(a_last[:, 2 * pair:2 * pair + 1]),
                        jnp.exp2(a_last[:, 2 * pair + 1:2 * pair + 2]))
        ms_scr[pair] = state * dec + upd
        return y

    b_queue = []
    for st in range(n_sub):
        rs = slice(st * sub, (st + 1) * sub)
        x_in, b_in, c_in = x_all[rs], b_all[rs], c_all[rs]
        dt = _softplus(dt_raw[rs] + dtb_ref[...])
        a_cum = _cumsum_rows(dt * a_log2)
        a_cum_t = _transpose_f32(a_cum)
        dt_t = _transpose_f32(dt)
        a_last = a_cum[sub - 1:sub]
        a_src_t = a_cum_t - jnp.log2(dt_t)
        w_end_t = dt_t * jnp.exp2(a_cum_t[:, sub - 1:sub] - a_cum_t)
        e_cum = jnp.exp2(a_cum)

        counts = _spread(len(b_queue) + len(out_queue), n_m2_pairs)
        y_pairs = []
        for g in range(M2_GROUPS):
            b_g = b_in[:, g * M2_STATE:(g + 1) * M2_STATE]
            c_g = c_in[:, g * M2_STATE:(g + 1) * M2_STATE]
            cb = _dot_nt(c_g.astype(BF16), b_g.astype(BF16))
            b_gt = _transpose_f32(b_g)
            for p in range(pairs_per_group):
                pair = g * pairs_per_group + p
                for _ in range(counts[pair]):
                    (out_queue or b_queue).pop(0)()
                y_pairs.append(m2_pair_step(pair, c_g, cb, b_gt, x_in, a_cum, a_src_t, w_end_t, e_cum, a_last))
        y = jnp.concatenate(y_pairs, axis=1)
        y = (y + d_ref[...] * x_in) * _silu(mp_scr[rs, 0:width])
        gw = width // M2_GROUPS
        for g in range(M2_GROUPS):
            yg = y[:, g * gw:(g + 1) * gw]
            yg = yg * lax.rsqrt(jnp.mean(yg * yg, axis=-1, keepdims=True) + EPS)
            mix_scr[rs, key + g * gw:key + (g + 1) * gw] = (yg * mnw_ref[:, g * gw:(g + 1) * gw]).astype(BF16)
        b_queue += [functools.partial(out_piece_b, rs, c0, c1) for c0, c1 in out_cols]
    for piece in out_queue + b_queue:
        piece()

    if emit_state:
        @pl.when(t_idx == pl.num_programs(1) - 1)
        def _():
            hsfin_ref[...] = hs_scr[...]
            msfin_ref[...] = ms_scr[...]
            halofin_ref[...] = halo_scr[...]


def _mixer(x, norm1_w, w_in, lb_logits, hg_norm_w, conv_w, conv_b, dt_bias, a_log, d_skip, m2_norm_w, w_out,
           hg_s0, m2_s0, halo0, *, tile, emit_state):
    bsz, length, d = x.shape
    key = lb_logits.shape[-1]
    width = m2_norm_w.shape[-1]
    xbc_w = conv_w.shape[-1]
    m2_cols = w_in.shape[-1] - 4 * key
    sub = min(SUB_TILE, tile)
    const2 = lambda b, t: (0, 0)
    const3 = lambda b, t: (0, 0, 0)
    row_spec = pl.BlockSpec((None, tile, d), lambda b, t: (b, t, 0))
    out_shape = [jax.ShapeDtypeStruct((bsz, length, d), F32)]
    out_specs = [row_spec]
    if emit_state:
        out_shape += [jax.ShapeDtypeStruct(a.shape, F32) for a in (hg_s0, m2_s0, halo0)]
        out_specs += [pl.BlockSpec(hg_s0.shape, const3), pl.BlockSpec(m2_s0.shape, const3),
                      pl.BlockSpec(halo0.shape, const2)]
    res = pl.pallas_call(
        functools.partial(_mixer_kernel, tile=tile, sub=sub, emit_state=emit_state),
        grid=(bsz, length // tile),
        in_specs=[
            row_spec,
            pl.BlockSpec((1, d), const2),
            pl.BlockSpec(w_in.shape, const2, pipeline_mode=pl.Buffered(1)),
            pl.BlockSpec(lb_logits.shape, const2),
            pl.BlockSpec((1, key), const2),
            pl.BlockSpec(conv_w.shape, const2),
            pl.BlockSpec((1, xbc_w), const2),
            pl.BlockSpec((1, DT_PAD), const2),
            pl.BlockSpec((1, DT_PAD), const2),
            pl.BlockSpec((1, width), const2),
            pl.BlockSpec((1, width), const2),
            pl.BlockSpec(w_out.shape, const2, pipeline_mode=pl.Buffered(1)),
            pl.BlockSpec(hg_s0.shape, const3),
            pl.BlockSpec(m2_s0.shape, const3),
            pl.BlockSpec(halo0.shape, const2),
        ],
        out_specs=out_specs,
        out_shape=out_shape,
        scratch_shapes=[pltpu.VMEM(hg_s0.shape, F32), pltpu.VMEM(m2_s0.shape, F32),
                        pltpu.VMEM(halo0.shape, F32),
                        pltpu.VMEM((tile, key + width), BF16),
                        pltpu.VMEM((2, tile, 4 * 2 * HG_DK), F32),
                        pltpu.VMEM((tile, m2_cols), F32)],
        compiler_params=pltpu.CompilerParams(
            dimension_semantics=("arbitrary", "arbitrary"), vmem_limit_bytes=VMEM_LIMIT),
        name="mixer",
    )(x, norm1_w.reshape(1, d), w_in, lb_logits, hg_norm_w.reshape(1, key), conv_w, conv_b.reshape(1, xbc_w),
      dt_bias, a_log, d_skip, m2_norm_w.reshape(1, width), w_out, hg_s0, m2_s0, halo0)
    return res if emit_state else res[0]


def _ffn_kernel(h_ref, n2_ref, wup_ref, cw_ref, cb_ref, wdn_ref, fn_ref, halo0_ref, o_ref,
                halo_scr, act, *, tile, col_chunk):
    t_idx = pl.program_id(1)
    d_ff = wdn_ref.shape[0]

    @pl.when(t_idx == 0)
    def _():
        halo_scr[...] = halo0_ref[...]

    h = h_ref[...]
    u_in = _rmsnorm(h, n2_ref[...]).astype(BF16)

    def conv_cols(c0, c1):
        raw = _dot(u_in, wup_ref[:, c0:c1])
        out = _causal_conv(raw, halo_scr[:, c0:c1], cw_ref, cb_ref, c0, c1, FFN_CONV)
        halo_scr[:, c0:c1] = raw[tile - HALO:tile]
        return out

    for c0 in range(0, d_ff, col_chunk):
        c1 = min(d_ff, c0 + col_chunk)
        gate = conv_cols(c0, c1)
        val = conv_cols(d_ff + c0, d_ff + c1)
        act[:, c0:c1] = (_silu(gate) * val).astype(BF16)

    h2 = h + _dot(act[...], wdn_ref[...])
    o_ref[...] = _rmsnorm(h2, fn_ref[...])


def _ffn(h, norm2_w, w_up, conv_w, conv_b, w_down, final_w, halo0, *, tile):
    bsz, length, d = h.shape
    d_ff = w_down.shape[0]
    const2 = lambda b, t: (0, 0)
    return pl.pallas_call(
        functools.partial(_ffn_kernel, tile=tile, col_chunk=PIECE_COLS),
        grid=(bsz, length // tile),
        in_specs=[
            pl.BlockSpec((None, tile, d), lambda b, t: (b, t, 0)),
            pl.BlockSpec((1, d), const2),
            pl.BlockSpec(w_up.shape, const2, pipeline_mode=pl.Buffered(1)),
            pl.BlockSpec(conv_w.shape, const2),
            pl.BlockSpec((1, 2 * d_ff), const2),
            pl.BlockSpec(w_down.shape, const2, pipeline_mode=pl.Buffered(1)),
            pl.BlockSpec((1, d), const2),
            pl.BlockSpec(halo0.shape, const2),
        ],
        out_specs=pl.BlockSpec((None, tile, d), lambda b, t: (b, t, 0)),
        out_shape=jax.ShapeDtypeStruct((bsz, length, d), F32),
        scratch_shapes=[pltpu.VMEM((HALO, 2 * d_ff), F32),
                        pltpu.VMEM((tile, d_ff), BF16)],
        compiler_params=pltpu.CompilerParams(
            dimension_semantics=("arbitrary", "arbitrary"), vmem_limit_bytes=VMEM_LIMIT),
        name="ffn",
    )(h, norm2_w.reshape(1, d), w_up, conv_w, conv_b.reshape(1, 2 * d_ff), w_down,
      final_w.reshape(1, d), halo0)


def _pick_tile(length, pref):
    t = pref
    while length % t:
        t //= 2
    return t


def kernel(x, meta_tokens, norm1_w, w_in, hg_lb_logits, hg_norm_w, m2_conv_w, m2_conv_b, m2_dt_bias,
           m2_a_log, m2_d, m2_norm_w, w_out, norm2_w, ffn_w_up, ffn_conv_w, ffn_conv_b, ffn_w_down,
           final_norm_w):
    assert norm1_w.shape[0] == 1, "one layer"
    bsz, seq, d = x.shape
    hg_key = hg_lb_logits.shape[-1]
    m2_width = m2_norm_w.shape[-1]
    n_m2_heads = m2_dt_bias.shape[-1]
    xbc_w = m2_conv_w.shape[-1]
    assert hg_key == hg_norm_w.shape[-1], "q/f and i/g column groups have the same width"
    assert w_in.shape[-1] == 4 * hg_key + m2_width + xbc_w + n_m2_heads

    pad = DT_PAD - n_m2_heads
    w_in_b = jnp.pad(w_in[0].astype(BF16), ((0, 0), (0, pad)))
    dt_bias_p = jnp.pad(m2_dt_bias[0], (0, pad)).reshape(1, DT_PAD)
    a_log_p = jnp.pad(m2_a_log[0], (0, pad)).reshape(1, DT_PAD)
    d_skip_e = jnp.repeat(m2_d[0], M2_HEADDIM).reshape(1, m2_width)
    w_out_b = w_out[0].astype(BF16)
    w_up_b = ffn_w_up[0].astype(BF16)
    w_down_b = ffn_w_down[0].astype(BF16)

    mixer = functools.partial(_mixer, norm1_w=norm1_w[0], w_in=w_in_b, lb_logits=hg_lb_logits,
                              hg_norm_w=hg_norm_w[0], conv_w=m2_conv_w[0], conv_b=m2_conv_b[0],
                              dt_bias=dt_bias_p, a_log=a_log_p, d_skip=d_skip_e, m2_norm_w=m2_norm_w[0],
                              w_out=w_out_b)

    meta = meta_tokens.astype(x.dtype)[None]
    zeros_hg = jnp.zeros((hg_key // HG_DK, HG_DK, HG_DV), F32)
    zeros_m2 = jnp.zeros((m2_width // LANES, M2_STATE, LANES), F32)
    zeros_halo = jnp.zeros((HALO, xbc_w), F32)
    h1_meta, hg_s0, m2_s0, m2_halo0 = mixer(meta, hg_s0=zeros_hg, m2_s0=zeros_m2, halo0=zeros_halo,
                                            tile=N_META, emit_state=True)
    u_meta = _norm_proj(h1_meta, norm2_w[0], w_up_b, tile=N_META)
    ffn_halo0 = u_meta[0, N_META - HALO:, :]

    h1 = mixer(x, hg_s0=hg_s0, m2_s0=m2_s0, halo0=m2_halo0, tile=_pick_tile(seq, MIXER_TILE), emit_state=False)
    return _ffn(h1, norm2_w[0], w_up_b, ffn_conv_w[0], ffn_conv_b[0], w_down_b, final_norm_w,
                ffn_halo0, tile=_pick_tile(seq, FFN_TILE))
```

```python
import functools
import math

import jax
import jax.numpy as jnp
from jax import lax
from jax.experimental import pallas as pl
from jax.experimental.pallas import tpu as pltpu

F32 = jnp.float32
BF16 = jnp.bfloat16

EPS = 1e-6
N_META = 16
LANES = 128
SUBLANES = 8
HALO = SUBLANES
HG_DK = 128
HG_DV = 128
HG_DIAG = 32
M2_HEADDIM = 64
M2_STATE = 128
M2_GROUPS = 2
M2_CONV = 4
FFN_CONV = 3
DT_PAD = LANES
SUB_TILE = 128
MIXER_TILE = 256
FFN_TILE = 512
FFN_SUB = 256
PIECE_COLS = 256
VMEM_LIMIT = 56 * 1024 * 1024
LOG2E = math.log2(math.e)
NEG_BIG = -1e30


def _silu(x):
    hx = 0.5 * x
    return hx + hx * jnp.tanh(hx)


def _softplus(x):
    return jnp.maximum(x, 0.0) + jnp.log1p(jnp.exp(-jnp.abs(x)))


def _rmsnorm(x, w):
    ms = jnp.mean(x * x, axis=-1, keepdims=True)
    return x * lax.rsqrt(ms + EPS) * w


def _dot(a, b):
    return jnp.dot(a, b, preferred_element_type=F32)


def _dot_nt(a, b):
    return lax.dot_general(a, b, (((1,), (1,)), ((), ())), preferred_element_type=F32)


def _dot_tn(a, b):
    return lax.dot_general(a, b, (((0,), (0,)), ((), ())), preferred_element_type=F32)


def _split3(x):
    x1 = x.astype(BF16)
    r1 = x - x1.astype(F32)
    x2 = r1.astype(BF16)
    x3 = (r1 - x2.astype(F32)).astype(BF16)
    return x1, x2, x3


def _cumsum_rows(x):
    t = x.shape[0]
    r = lax.broadcasted_iota(jnp.int32, (t, t), 0)
    c = lax.broadcasted_iota(jnp.int32, (t, t), 1)
    tril = jnp.where(c <= r, 1.0, 0.0).astype(BF16)
    p1, p2, p3 = _split3(x)
    return _dot(tril, p1) + _dot(tril, p2) + _dot(tril, p3)


def _eye_bf16():
    r = lax.broadcasted_iota(jnp.int32, (LANES, LANES), 0)
    c = lax.broadcasted_iota(jnp.int32, (LANES, LANES), 1)
    return jnp.where(r == c, 1.0, 0.0).astype(BF16)


def _transpose_f32(x):
    if x.shape[0] == x.shape[1]:
        return jnp.transpose(x)
    eye = _eye_bf16()
    p1, p2, p3 = _split3(x)
    return _dot_nt(eye, p1) + _dot_nt(eye, p2) + _dot_nt(eye, p3)


def _block_ref_rows(b, block, row_in_block):
    t = b.shape[0]
    parts = [jnp.broadcast_to(b[a + row_in_block:a + row_in_block + 1], (block, b.shape[1]))
             for a in range(0, t, block)]
    return parts[0] if len(parts) == 1 else jnp.concatenate(parts, axis=0)


def _shift_rows(x, prev_row):
    rolled = pltpu.roll(x, 1, axis=0)
    first = lax.broadcasted_iota(jnp.int32, (SUBLANES, x.shape[1]), 0) == 0
    top = jnp.where(first, prev_row, rolled[0:SUBLANES])
    if x.shape[0] == SUBLANES:
        return top
    return jnp.concatenate([top, rolled[SUBLANES:]], axis=0)


def _causal_conv(x, halo, w_ref, b_ref, c0, c1, taps):
    acc = b_ref[:, c0:c1] + x * w_ref[taps - 1:taps, c0:c1]
    xs = x
    for j in range(1, taps):
        xs = _shift_rows(xs, halo[HALO - j:HALO - j + 1])
        acc = acc + xs * w_ref[taps - 1 - j:taps - j, c0:c1]
    return acc


def _spread(n_items, n_slots):
    return [(n_items * (i + 1) + n_slots - 1) // n_slots - (n_items * i + n_slots - 1) // n_slots
            for i in range(n_slots)]


def _norm_proj_kernel(x_ref, nw_ref, w_ref, o_ref, *, col_chunk):
    u = _rmsnorm(x_ref[...], nw_ref[...]).astype(BF16)
    n = o_ref.shape[-1]
    for c0 in range(0, n, col_chunk):
        c1 = min(n, c0 + col_chunk)
        o_ref[:, c0:c1] = _dot(u, w_ref[:, c0:c1])


def _norm_proj(x, norm_w, w, *, tile):
    bsz, length, d = x.shape
    n = w.shape[1]
    return pl.pallas_call(
        functools.partial(_norm_proj_kernel, col_chunk=512),
        grid=(bsz, length // tile),
        in_specs=[
            pl.BlockSpec((None, tile, d), lambda b, t: (b, t, 0)),
            pl.BlockSpec((1, d), lambda b, t: (0, 0)),
            pl.BlockSpec((d, n), lambda b, t: (0, 0), pipeline_mode=pl.Buffered(1)),
        ],
        out_specs=pl.BlockSpec((None, tile, n), lambda b, t: (b, t, 0)),
        out_shape=jax.ShapeDtypeStruct((bsz, length, n), F32),
        compiler_params=pltpu.CompilerParams(
            dimension_semantics=("arbitrary", "arbitrary"), vmem_limit_bytes=VMEM_LIMIT),
        name="norm_proj",
    )(x, norm_w.reshape(1, d), w)


def _hgrn2_scores(b, q, k, levels, second_half, level_mask, diag, diag_mask):
    ref = _block_ref_rows(b, diag, diag // 2 - 1)
    d = b - ref
    s_d = _dot_nt((q * jnp.exp2(d)).astype(BF16), (k * jnp.exp2(-d)).astype(BF16))
    scores = jnp.where(diag_mask, s_d, 0.0)
    for n in levels:
        ref = _block_ref_rows(b, n, n // 2 - 1)
        y = (jnp.exp2(-jnp.abs(b - ref)) * jnp.where(second_half[n], q, k)).astype(BF16)
        scores = jnp.where(level_mask[n], _dot_nt(y, y), scores)
    return scores


def _mixer_kernel(x_ref, n1_ref, w_ref, wdt_ref, lbl_ref, hnw_ref, cw_ref, cb_ref, dtb_ref, alog_ref, d_ref, mnw_ref,
                  wo_ref, hs0_ref, ms0_ref, halo0_ref, o_ref, *rest, tile, sub, emit_state):
    if emit_state:
        hsfin_ref, msfin_ref, halofin_ref, hs_scr, ms_scr, halo_scr, mix_scr, pr_scr, mp_scr = rest
    else:
        hs_scr, ms_scr, halo_scr, mix_scr, pr_scr, mp_scr = rest
    t_idx = pl.program_id(1)

    @pl.when(t_idx == 0)
    def _():
        hs_scr[...] = hs0_ref[...]
        ms_scr[...] = ms0_ref[...]
        halo_scr[...] = halo0_ref[...]

    key = lbl_ref.shape[-1]
    pw = 2 * HG_DK
    n_hg_pairs = key // pw
    width = mnw_ref.shape[-1]
    xbc_w = cw_ref.shape[-1]
    gn = (xbc_w - width) // 2
    n_m2_pairs = width // LANES
    pairs_per_group = n_m2_pairs // M2_GROUPS
    col_m2 = 4 * key
    n_sub = tile // sub
    stacked = sub % LANES == 0

    x = x_ref[...]
    u = _rmsnorm(x, n1_ref[...]).astype(BF16)

    def hg_piece(p, j):
        c0 = j * key + p * pw
        pr_scr[p % 2, :, j * pw:(j + 1) * pw] = _dot(u, w_ref[:, c0:c0 + pw])

    def m2_piece(c0, c1):
        mp_scr[:, c0:c1] = _dot(u, w_ref[:, col_m2 + c0:col_m2 + c1])

    def dt_piece():
        mp_scr[:, width + xbc_w:] = _dot(u, wdt_ref[...])

    xbc_pieces = [functools.partial(m2_piece, c0, c0 + PIECE_COLS)
                  for c0 in range(width, width + xbc_w, PIECE_COLS)] + [dt_piece]
    z_pieces = [functools.partial(m2_piece, c0, c0 + PIECE_COLS) for c0 in range(0, width, PIECE_COLS)]

    def out_piece_a(c0, c1):
        o_ref[:, c0:c1] = x[:, c0:c1] + _dot(mix_scr[:, 0:key], wo_ref[0:key, c0:c1])

    def out_piece_b(rs, c0, c1):
        o_ref[rs, c0:c1] = o_ref[rs, c0:c1] + _dot(mix_scr[rs, key:], wo_ref[key:, c0:c1])

    d_model = o_ref.shape[-1]
    out_cols = [(c0, c0 + PIECE_COLS) for c0 in range(0, d_model, PIECE_COLS)]

    logits = lbl_ref[...]
    e = jnp.exp(logits - jnp.max(logits, axis=0, keepdims=True))
    lb = e[0:1] / jnp.sum(e, axis=0, keepdims=True)
    f_c0 = 0.5 * (1.0 + lb)
    f_c1 = 0.5 * (1.0 - lb)

    row = lax.broadcasted_iota(jnp.int32, (sub, HG_DK), 0)
    r2 = lax.broadcasted_iota(jnp.int32, (sub, sub), 0)
    c2 = lax.broadcasted_iota(jnp.int32, (sub, sub), 1)
    diag = min(HG_DIAG, sub)
    levels = []
    n = sub
    while n > diag:
        levels.append(n)
        n //= 2
    second_half = {n: (row & (n - 1)) >= n // 2 for n in levels}
    level_mask = {n: ((r2 & -n) == (c2 & -n)) & ((r2 & (n - 1)) >= n // 2) & ((c2 & (n - 1)) < n // 2)
                  for n in levels}
    diag_mask = ((r2 & -diag) == (c2 & -diag)) & (c2 <= r2)

    def hg_scores_stage(st, hh, b2, q2, k2, v2):
        rs = slice(st * sub, (st + 1) * sub)
        hs = slice(hh * HG_DK, (hh + 1) * HG_DK)
        b, q, k, v = b2[:, hs], q2[rs, hs], k2[rs, hs], v2[rs, hs]
        scores = _hgrn2_scores(b, q, k, levels, second_half, level_mask, diag, diag_mask).astype(BF16)
        b_last = b[sub - 1:sub]
        q_dec = (q * jnp.exp2(b)).astype(BF16)
        k_end = (k * jnp.exp2(b_last - b)).astype(BF16)
        decay = jnp.transpose(jnp.broadcast_to(jnp.exp2(b_last), (HG_DK, HG_DK)))
        return scores, q_dec, k_end, v, decay

    def hg_state_stage(p, st, hh, staged, gate2):
        scores, q_dec, k_end, v, decay = staged
        rs = slice(st * sub, (st + 1) * sub)
        head = 2 * p + hh
        hs = slice(hh * HG_DK, (hh + 1) * HG_DK)
        osl = slice(head * HG_DV, (head + 1) * HG_DV)
        state = hs_scr[head]
        if stacked:
            o = _dot(jnp.concatenate([scores, q_dec], axis=1),
                     jnp.concatenate([v, state.astype(BF16)], axis=0))
        else:
            o = _dot(scores, v) + _dot(q_dec, state.astype(BF16))
        hs_scr[head] = state * decay + _dot_tn(k_end, v)
        o = o * lax.rsqrt(jnp.mean(o * o, axis=-1, keepdims=True) + EPS)
        mix_scr[rs, osl] = (o * hnw_ref[:, osl] * gate2[rs, hs]).astype(BF16)

    def m2_decays(st):
        dt = _softplus(mp_scr[st * sub:(st + 1) * sub, width + xbc_w:] + dtb_ref[...])
        a_log2 = -jnp.exp(alog_ref[...]) * LOG2E
        a_cum = _cumsum_rows(dt * a_log2)
        a_cum_t = _transpose_f32(a_cum)
        dt_t = _transpose_f32(dt)
        a_last = a_cum[sub - 1:sub]
        a_src_t = a_cum_t - jnp.log2(dt_t)
        w_end_t = dt_t * jnp.exp2(a_cum_t[:, sub - 1:sub] - a_cum_t)
        return a_cum, a_src_t, w_end_t, jnp.exp2(a_cum), a_last

    for j in range(4):
        hg_piece(0, j)
    m2_queue = xbc_pieces + z_pieces
    m2_share = _spread(len(m2_queue), n_hg_pairs)
    issued = [sum(m2_share[:p + 1]) for p in range(n_hg_pairs)]
    dt_ready_pair = min(p for p in range(n_hg_pairs) if issued[p] >= len(xbc_pieces))
    def hg_prep(p):
        pr = pr_scr.at[p % 2]
        cs = slice(p * pw, (p + 1) * pw)
        f = f_c0[:, cs] + f_c1[:, cs] * jnp.tanh(0.5 * pr[:, pw:2 * pw])
        k2 = 1.0 - f
        q2 = _silu(pr[:, 0:pw])
        v2 = pr[:, 2 * pw:3 * pw].astype(BF16)
        gate2 = _silu(pr[:, 3 * pw:4 * pw])
        b_tile = _cumsum_rows(jnp.log2(f))
        b_subs = []
        for st in range(n_sub):
            b2 = b_tile[st * sub:(st + 1) * sub]
            b_subs.append(b2 - b_tile[st * sub - 1:st * sub] if st else b2)
        return b_subs, q2, k2, v2, gate2

    prepped = hg_prep(0)
    for p in range(n_hg_pairs):
        queue = [functools.partial(hg_piece, p + 1, j) for j in range(4)] if p + 1 < n_hg_pairs else []
        for _ in range(m2_share[p]):
            queue.append(m2_queue.pop(0))
        counts = _spread(len(queue), 2 * n_sub)
        b_subs, q2, k2, v2, gate2 = prepped
        step = 0
        staged = {}
        for st in range(n_sub):
            for hh in range(2):
                for _ in range(counts[step]):
                    queue.pop(0)()
                step += 1
                staged[st, hh] = hg_scores_stage(st, hh, b_subs[st], q2, k2, v2)
        if p + 1 < n_hg_pairs:
            prepped = hg_prep(p + 1)
        for st in range(n_sub):
            for hh in range(2):
                hg_state_stage(p, st, hh, staged[st, hh], gate2)
        if p == dt_ready_pair:
            decays = [m2_decays(st) for st in range(n_sub)]

    conv_chunk = 2 * PIECE_COLS
    conv_starts = list(range(0, xbc_w, conv_chunk))
    out_share = _spread(len(out_cols), len(conv_starts))
    out_queue = [functools.partial(out_piece_a, c0, c1) for c0, c1 in out_cols]
    xbc_parts = []
    for i, c0 in enumerate(conv_starts):
        for _ in range(out_share[i]):
            out_queue.pop(0)()
        c1 = min(xbc_w, c0 + conv_chunk)
        raw = mp_scr[:, width + c0:width + c1]
        xbc_parts.append(_silu(_causal_conv(raw, halo_scr[:, c0:c1], cw_ref, cb_ref, c0, c1, M2_CONV)))
        halo_scr[:, c0:c1] = raw[tile - HALO:tile]
    xbc = jnp.concatenate(xbc_parts, axis=1)
    x_all = xbc[:, 0:width]
    b_all = xbc[:, width:width + gn]
    c_all = xbc[:, width + gn:]

    causal_bias = jnp.where(c2 <= r2, 0.0, NEG_BIG)
    lane = lax.broadcasted_iota(jnp.int32, (1, LANES), 1)
    lo = lane < M2_HEADDIM

    def m2_pair_prep(pair, c_g, cb, b_gt, x_in, a_cum, a_src_t, w_end_t, e_cum, a_last):
        xp = x_in[:, pair * LANES:(pair + 1) * LANES]
        lhs, xms, upd_l = [], [], []
        for half in range(2):
            h = 2 * pair + half
            keep = lo if half == 0 else jnp.logical_not(lo)
            xms.append(jnp.where(keep, xp, 0.0).astype(BF16))
            seg = (a_cum[:, h:h + 1] + causal_bias) - a_src_t[h:h + 1, :]
            lhs += [(cb * jnp.exp2(seg)).astype(BF16), (c_g * e_cum[:, h:h + 1]).astype(BF16)]
            upd_l.append((b_gt * w_end_t[h:h + 1, :]).astype(BF16))
        dec = jnp.where(lo, jnp.exp2(a_last[:, 2 * pair:2 * pair + 1]),
                        jnp.exp2(a_last[:, 2 * pair + 1:2 * pair + 2]))
        return lhs, xms, upd_l, dec

    def m2_pair_state(pair, prep):
        lhs, xms, upd_l, dec = prep
        state = ms_scr[pair]
        sms = [jnp.where(lo, state, 0.0).astype(BF16), jnp.where(lo, 0.0, state).astype(BF16)]
        rhs = [xms[0], sms[0], xms[1], sms[1]]
        if stacked:
            y = _dot(jnp.concatenate(lhs, axis=1), jnp.concatenate(rhs, axis=0))
            upd = _dot(jnp.concatenate(upd_l, axis=1), jnp.concatenate(xms, axis=0))
        else:
            y = sum(_dot(a, b) for a, b in zip(lhs, rhs))
            upd = sum(_dot(a, b) for a, b in zip(upd_l, xms))
        ms_scr[pair] = state * dec + upd
        return y

    b_queue = []
    for st in range(n_sub):
        rs = slice(st * sub, (st + 1) * sub)
        x_in, b_in, c_in = x_all[rs], b_all[rs], c_all[rs]
        a_cum, a_src_t, w_end_t, e_cum, a_last = decays[st]

        counts = _spread(len(b_queue) + len(out_queue), n_m2_pairs)
        preps = []
        for g in range(M2_GROUPS):
            b_g = b_in[:, g * M2_STATE:(g + 1) * M2_STATE]
            c_g = c_in[:, g * M2_STATE:(g + 1) * M2_STATE]
            cb = _dot_nt(c_g.astype(BF16), b_g.astype(BF16))
            b_gt = _transpose_f32(b_g)
            for p in range(pairs_per_group):
                pair = g * pairs_per_group + p
                for _ in range(counts[pair]):
                    (out_queue or b_queue).pop(0)()
                preps.append(m2_pair_prep(pair, c_g, cb, b_gt, x_in, a_cum, a_src_t, w_end_t, e_cum, a_last))
        y_pairs = [m2_pair_state(pair, preps[pair]) for pair in range(n_m2_pairs)]
        y = jnp.concatenate(y_pairs, axis=1)
        y = (y + d_ref[...] * x_in) * _silu(mp_scr[rs, 0:width])
        gw = width // M2_GROUPS
        for g in range(M2_GROUPS):
            yg = y[:, g * gw:(g + 1) * gw]
            yg = yg * lax.rsqrt(jnp.mean(yg * yg, axis=-1, keepdims=True) + EPS)
            mix_scr[rs, key + g * gw:key + (g + 1) * gw] = (yg * mnw_ref[:, g * gw:(g + 1) * gw]).astype(BF16)
        b_queue += [functools.partial(out_piece_b, rs, c0, c1) for c0, c1 in out_cols]
    for piece in out_queue + b_queue:
        piece()

    if emit_state:
        @pl.when(t_idx == pl.num_programs(1) - 1)
        def _():
            hsfin_ref[...] = hs_scr[...]
            msfin_ref[...] = ms_scr[...]
            halofin_ref[...] = halo_scr[...]


def _mixer(x, norm1_w, w_in, w_dt, lb_logits, hg_norm_w, conv_w, conv_b, dt_bias, a_log, d_skip, m2_norm_w, w_out,
           hg_s0, m2_s0, halo0, *, tile, emit_state):
    bsz, length, d = x.shape
    key = lb_logits.shape[-1]
    width = m2_norm_w.shape[-1]
    xbc_w = conv_w.shape[-1]
    m2_cols = w_in.shape[-1] - 4 * key + DT_PAD
    assert w_in.shape[-1] == 4 * key + width + xbc_w and xbc_w % PIECE_COLS == 0
    sub = min(SUB_TILE, tile)
    const2 = lambda b, t: (0, 0)
    const3 = lambda b, t: (0, 0, 0)
    row_spec = pl.BlockSpec((None, tile, d), lambda b, t: (b, t, 0))
    out_shape = [jax.ShapeDtypeStruct((bsz, length, d), F32)]
    out_specs = [row_spec]
    if emit_state:
        out_shape += [jax.ShapeDtypeStruct(a.shape, F32) for a in (hg_s0, m2_s0, halo0)]
        out_specs += [pl.BlockSpec(hg_s0.shape, const3), pl.BlockSpec(m2_s0.shape, const3),
                      pl.BlockSpec(halo0.shape, const2)]
    res = pl.pallas_call(
        functools.partial(_mixer_kernel, tile=tile, sub=sub, emit_state=emit_state),
        grid=(bsz, length // tile),
        in_specs=[
            row_spec,
            pl.BlockSpec((1, d), const2),
            pl.BlockSpec(w_in.shape, const2, pipeline_mode=pl.Buffered(1)),
            pl.BlockSpec(w_dt.shape, const2),
            pl.BlockSpec(lb_logits.shape, const2),
            pl.BlockSpec((1, key), const2),
            pl.BlockSpec(conv_w.shape, const2),
            pl.BlockSpec((1, xbc_w), const2),
            pl.BlockSpec((1, DT_PAD), const2),
            pl.BlockSpec((1, DT_PAD), const2),
            pl.BlockSpec((1, width), const2),
            pl.BlockSpec((1, width), const2),
            pl.BlockSpec(w_out.shape, const2, pipeline_mode=pl.Buffered(1)),
            pl.BlockSpec(hg_s0.shape, const3),
            pl.BlockSpec(m2_s0.shape, const3),
            pl.BlockSpec(halo0.shape, const2),
        ],
        out_specs=out_specs,
        out_shape=out_shape,
        scratch_shapes=[pltpu.VMEM(hg_s0.shape, F32), pltpu.VMEM(m2_s0.shape, F32),
                        pltpu.VMEM(halo0.shape, F32),
                        pltpu.VMEM((tile, key + width), BF16),
                        pltpu.VMEM((2, tile, 4 * 2 * HG_DK), F32),
                        pltpu.VMEM((tile, m2_cols), F32)],
        compiler_params=pltpu.CompilerParams(
            dimension_semantics=("arbitrary", "arbitrary"), vmem_limit_bytes=VMEM_LIMIT),
        name="mixer",
    )(x, norm1_w.reshape(1, d), w_in, w_dt, lb_logits, hg_norm_w.reshape(1, key), conv_w, conv_b.reshape(1, xbc_w),
      dt_bias, a_log, d_skip, m2_norm_w.reshape(1, width), w_out, hg_s0, m2_s0, halo0)
    return res if emit_state else res[0]


def _ffn_kernel(h_ref, n2_ref, wup_ref, cw_ref, cb_ref, wdn_ref, fn_ref, halo0_ref, o_ref,
                halo_scr, act, *, tile, sub, col_chunk):
    t_idx = pl.program_id(1)
    d_ff = wdn_ref.shape[0]

    @pl.when(t_idx == 0)
    def _():
        halo_scr[...] = halo0_ref[...]

    d_model = o_ref.shape[-1]
    chunks = [(c0, min(d_ff, c0 + col_chunk)) for c0 in range(0, d_ff, col_chunk)]
    out_cols = [(c0, c0 + col_chunk) for c0 in range(0, d_model, col_chunk)]

    def down_piece(rs, c0, c1):
        o_ref[rs, c0:c1] = h_ref[rs, c0:c1] + _dot(act[rs, :], wdn_ref[:, c0:c1])

    def finish(rs):
        o_ref[rs, :] = _rmsnorm(o_ref[rs, :], fn_ref[...])

    pending = []
    for r0 in range(0, tile, sub):
        rs = slice(r0, r0 + sub)
        u_in = _rmsnorm(h_ref[rs, :], n2_ref[...]).astype(BF16)
        counts = _spread(len(pending), len(chunks))

        def conv_cols(c0, c1):
            raw = _dot(u_in, wup_ref[:, c0:c1])
            out = _causal_conv(raw, halo_scr[:, c0:c1], cw_ref, cb_ref, c0, c1, FFN_CONV)
            halo_scr[:, c0:c1] = raw[sub - HALO:sub]
            return out

        for i, (c0, c1) in enumerate(chunks):
            for _ in range(counts[i]):
                pending.pop(0)()
            gate = conv_cols(c0, c1)
            val = conv_cols(d_ff + c0, d_ff + c1)
            act[rs, c0:c1] = (_silu(gate) * val).astype(BF16)
        pending = [functools.partial(down_piece, rs, c0, c1) for c0, c1 in out_cols]
        pending.append(functools.partial(finish, rs))
    for piece in pending:
        piece()


def _ffn(h, norm2_w, w_up, conv_w, conv_b, w_down, final_w, halo0, *, tile):
    bsz, length, d = h.shape
    d_ff = w_down.shape[0]
    const2 = lambda b, t: (0, 0)
    return pl.pallas_call(
        functools.partial(_ffn_kernel, tile=tile, sub=min(FFN_SUB, tile), col_chunk=PIECE_COLS),
        grid=(bsz, length // tile),
        in_specs=[
            pl.BlockSpec((None, tile, d), lambda b, t: (b, t, 0)),
            pl.BlockSpec((1, d), const2),
            pl.BlockSpec(w_up.shape, const2, pipeline_mode=pl.Buffered(1)),
            pl.BlockSpec(conv_w.shape, const2),
            pl.BlockSpec((1, 2 * d_ff), const2),
            pl.BlockSpec(w_down.shape, const2, pipeline_mode=pl.Buffered(1)),
            pl.BlockSpec((1, d), const2),
            pl.BlockSpec(halo0.shape, const2),
        ],
        out_specs=pl.BlockSpec((None, tile, d), lambda b, t: (b, t, 0)),
        out_shape=jax.ShapeDtypeStruct((bsz, length, d), F32),
        scratch_shapes=[pltpu.VMEM((HALO, 2 * d_ff), F32),
                        pltpu.VMEM((tile, d_ff), BF16)],
        compiler_params=pltpu.CompilerParams(
            dimension_semantics=("arbitrary", "arbitrary"), vmem_limit_bytes=VMEM_LIMIT),
        name="ffn",
    )(h, norm2_w.reshape(1, d), w_up, conv_w, conv_b.reshape(1, 2 * d_ff), w_down,
      final_w.reshape(1, d), halo0)


def _pick_tile(length, pref):
    t = pref
    while length % t:
        t //= 2
    return t


def kernel(x, meta_tokens, norm1_w, w_in, hg_lb_logits, hg_norm_w, m2_conv_w, m2_conv_b, m2_dt_bias,
           m2_a_log, m2_d, m2_norm_w, w_out, norm2_w, ffn_w_up, ffn_conv_w, ffn_conv_b, ffn_w_down,
           final_norm_w):
    assert norm1_w.shape[0] == 1, "one layer"
    bsz, seq, d = x.shape
    hg_key = hg_lb_logits.shape[-1]
    m2_width = m2_norm_w.shape[-1]
    n_m2_heads = m2_dt_bias.shape[-1]
    xbc_w = m2_conv_w.shape[-1]
    assert hg_key == hg_norm_w.shape[-1], "q/f and i/g column groups have the same width"
    assert w_in.shape[-1] == 4 * hg_key + m2_width + xbc_w + n_m2_heads

    pad = DT_PAD - n_m2_heads
    n_main = w_in.shape[-1] - n_m2_heads
    w_in_b = w_in[0, :, :n_main].astype(BF16)
    w_dt_b = jnp.pad(w_in[0, :, n_main:], ((0, 0), (0, pad))).astype(BF16)
    dt_bias_p = jnp.pad(m2_dt_bias[0], (0, pad)).reshape(1, DT_PAD)
    a_log_p = jnp.pad(m2_a_log[0], (0, pad)).reshape(1, DT_PAD)
    d_skip_e = jnp.repeat(m2_d[0], M2_HEADDIM).reshape(1, m2_width)
    w_out_b = w_out[0].astype(BF16)
    w_up_b = ffn_w_up[0].astype(BF16)
    w_down_b = ffn_w_down[0].astype(BF16)

    mixer = functools.partial(_mixer, norm1_w=norm1_w[0], w_in=w_in_b, w_dt=w_dt_b, lb_logits=hg_lb_logits,
                              hg_norm_w=hg_norm_w[0], conv_w=m2_conv_w[0], conv_b=m2_conv_b[0],
                              dt_bias=dt_bias_p, a_log=a_log_p, d_skip=d_skip_e, m2_norm_w=m2_norm_w[0],
                              w_out=w_out_b)

    meta = meta_tokens.astype(x.dtype)[None]
    zeros_hg = jnp.zeros((hg_key // HG_DK, HG_DK, HG_DV), F32)
    zeros_m2 = jnp.zeros((m2_width // LANES, M2_STATE, LANES), F32)
    zeros_halo = jnp.zeros((HALO, xbc_w), F32)
    h1_meta, hg_s0, m2_s0, m2_halo0 = mixer(meta, hg_s0=zeros_hg, m2_s0=zeros_m2, halo0=zeros_halo,
                                            tile=N_META, emit_state=True)
    u_meta = _norm_proj(h1_meta, norm2_w[0], w_up_b, tile=N_META)
    ffn_halo0 = u_meta[0, N_META - HALO:, :]

    h1 = mixer(x, hg_s0=hg_s0, m2_s0=m2_s0, halo0=m2_halo0, tile=_pick_tile(seq, MIXER_TILE), emit_state=False)
    return _ffn(h1, norm2_w[0], w_up_b, ffn_conv_w[0], ffn_conv_b[0], w_down_b, final_norm_w,
                ffn_halo0, tile=_pick_tile(seq, FFN_TILE))
```
